```python
import jax, jax.numpy as jnp
from jax import lax
import numpy as np

D_MODEL = 2048
BATCH = 2
SEQ = 4096
DEPTH = 2
DEC_BATCH = 32
DEC_SEQ = 8
PAST_LEN = 8192
PAGE_SIZE = 128

LRU_WIDTH = D_MODEL // 2
LRU_BLOCKS = 16
LRU_BLOCK_W = LRU_WIDTH // LRU_BLOCKS
CONV_W = 4
LRU_C = 8.0
HEAD_DIM = D_MODEL // 16
SB_HEADS = 8
SB_WIDTH = SB_HEADS * HEAD_DIM
FOX_HEADS = 8
FOX_WIDTH = FOX_HEADS * HEAD_DIM
FOX_BIAS_INIT = 3.0
Q_BLOCK = 128
N_BRANCH = 3
D_FF = 4 * D_MODEL
RMS_EPS = 1e-6
NEG_INF = -1e30

OFF_LRU_X = 0
OFF_LRU_G = OFF_LRU_X + LRU_WIDTH
OFF_SB = OFF_LRU_G + LRU_WIDTH
OFF_FOX = OFF_SB + 3 * SB_WIDTH
OFF_FOX_F = OFF_FOX + 3 * FOX_WIDTH
OFF_GATE = OFF_FOX_F + FOX_HEADS
N_IN = OFF_GATE + N_BRANCH * D_MODEL

kernel_name = 'hawk_stickbreak_fox_parallel_decoder_step'


def _rms_norm(x, g):
    xf = x.astype(jnp.float32)
    y = xf * lax.rsqrt(jnp.mean(xf * xf, axis=-1, keepdims=True) + RMS_EPS)
    return (y * g.astype(jnp.float32)).astype(x.dtype)


def _gather_pages(pool, page_table):
    rows = pool[page_table]
    return rows.reshape(rows.shape[0], rows.shape[1] * rows.shape[2], *rows.shape[3:])


def _query_blocks(fn, q_arrays, qpos):
    t = qpos.shape[0]
    bq = min(Q_BLOCK, t)
    nb = t // bq

    def split(a):
        return jnp.moveaxis(a.reshape(a.shape[0], nb, bq, *a.shape[2:]), 1, 0)

    xs = tuple(split(a) for a in q_arrays) + (qpos.reshape(nb, bq),)
    out = lax.map(lambda args: fn(*args), xs)
    out = jnp.moveaxis(out, 0, 1)
    return out.reshape(out.shape[0], t, *out.shape[3:])


def _stick_breaking(q, k, v, qpos, kpos):
    scale = HEAD_DIM ** -0.5

    def block(qb, tpos):
        z = jnp.einsum('bqhd,bshd->bhqs', qb, k, preferred_element_type=jnp.float32) * scale
        before = kpos[None, :] < tpos[:, None]
        log_beta = jax.nn.log_sigmoid(z)
        log_keep = jnp.where(before, log_beta - z, 0.0)
        later = lax.cumsum(log_keep, axis=3, reverse=True) - log_keep
        w = jnp.where(before, jnp.exp(log_beta + later), 0.0)
        return jnp.einsum('bhqs,bshd->bqhd', w.astype(v.dtype), v)

    return _query_blocks(block, (q,), qpos)


def _forgetting_attention(q, k, v, cum_q, cum_k, qpos, kpos):
    scale = HEAD_DIM ** -0.5
    cum_k_h = jnp.moveaxis(cum_k, 2, 1)[:, :, None, :]

    def block(qb, cqb, tpos):
        z = jnp.einsum('bqhd,bshd->bhqs', qb, k, preferred_element_type=jnp.float32) * scale
        bias = jnp.moveaxis(cqb, 2, 1)[..., None] - cum_k_h
        causal = kpos[None, :] <= tpos[:, None]
        p = jax.nn.softmax(jnp.where(causal, z + bias, NEG_INF), axis=-1)
        return jnp.einsum('bhqs,bshd->bqhd', p.astype(v.dtype), v)

    return _query_blocks(block, (q, cum_q), qpos)


def _lin_comb(left, right):
    a1, b1 = left
    a2, b2 = right
    return a1 * a2, a2 * b1 + b2


def _rglru_branch(xr, gate_in, h0, conv0, conv_w, conv_b, wa, ba, wx, bx, lam):
    b, t, r = xr.shape
    xp = jnp.concatenate([conv0.astype(xr.dtype), xr], axis=1)
    xc = conv_b + xp[:, 0:t] * conv_w[0]
    for tap in range(1, CONV_W):
        xc = xc + xp[:, tap:tap + t] * conv_w[tap]
    conv_new = xp[:, t:]
    xg = xc.reshape(b, t, LRU_BLOCKS, LRU_BLOCK_W)
    rg = jax.nn.sigmoid(jnp.einsum('btgi,gij->btgj', xg, wa).reshape(b, t, r) + ba)
    ig = jax.nn.sigmoid(jnp.einsum('btgi,gij->btgj', xg, wx).reshape(b, t, r) + bx)
    log_a = LRU_C * rg.astype(jnp.float32) * jax.nn.log_sigmoid(lam.astype(jnp.float32))
    a = jnp.exp(log_a)
    mult = jnp.sqrt(-jnp.expm1(2.0 * log_a))
    bt = mult * (ig * xc).astype(jnp.float32)
    bt = bt.at[:, 0].add(a[:, 0] * h0.astype(jnp.float32))
    _, h = lax.associative_scan(_lin_comb, (a, bt), axis=1)
    y = h.astype(xr.dtype) * jax.nn.gelu(gate_in)
    return y, h[:, -1], conv_new


def _layer(x, past_sb_k, past_sb_v, past_fox_k, past_fox_v, past_fox_logf, h0, conv0,
           norm1_g, w_in, b_gate, conv_w, conv_b, lru_wa, lru_ba, lru_wx, lru_bx, lru_lambda,
           fox_bf, w_br_lru, w_br_sb, w_br_fox, w_o, norm2_g, w_up, w_down):
    b, t, _ = x.shape
    p = past_sb_k.shape[1]
    qpos = p + jnp.arange(t)
    kpos = jnp.arange(p + t)
    xn = _rms_norm(x, norm1_g)
    u = xn @ w_in

    def heads(lo, n):
        return u[..., lo:lo + n * HEAD_DIM].reshape(b, t, n, HEAD_DIM)

    y_lru, h_last, conv_new = _rglru_branch(
        u[..., OFF_LRU_X:OFF_LRU_G], u[..., OFF_LRU_G:OFF_SB], h0, conv0,
        conv_w, conv_b, lru_wa, lru_ba, lru_wx, lru_bx, lru_lambda)

    sb_q = heads(OFF_SB, SB_HEADS)
    sb_k = heads(OFF_SB + SB_WIDTH, SB_HEADS)
    sb_v = heads(OFF_SB + 2 * SB_WIDTH, SB_HEADS)
    o_sb = _stick_breaking(sb_q, jnp.concatenate([past_sb_k, sb_k], axis=1),
                           jnp.concatenate([past_sb_v, sb_v], axis=1), qpos, kpos)

    fox_q = heads(OFF_FOX, FOX_HEADS)
    fox_k = heads(OFF_FOX + FOX_WIDTH, FOX_HEADS)
    fox_v = heads(OFF_FOX + 2 * FOX_WIDTH, FOX_HEADS)
    fox_logf = jax.nn.log_sigmoid((u[..., OFF_FOX_F:OFF_GATE] + fox_bf).astype(jnp.float32))
    cum = jnp.cumsum(jnp.concatenate([past_fox_logf.astype(jnp.float32), fox_logf], axis=1), axis=1)
    o_fox = _forgetting_attention(fox_q, jnp.concatenate([past_fox_k, fox_k], axis=1),
                                  jnp.concatenate([past_fox_v, fox_v], axis=1),
                                  cum[:, p:], cum, qpos, kpos)

    gates = jax.nn.sigmoid(u[..., OFF_GATE:] + b_gate).reshape(b, t, N_BRANCH, D_MODEL)
    merged = (gates[:, :, 0] * (y_lru @ w_br_lru)
              + gates[:, :, 1] * (o_sb.reshape(b, t, SB_WIDTH) @ w_br_sb)
              + gates[:, :, 2] * (o_fox.reshape(b, t, FOX_WIDTH) @ w_br_fox))
    x = x + merged @ w_o

    hn = _rms_norm(x, norm2_g)
    x = x + jnp.square(jax.nn.relu(hn @ w_up)) @ w_down
    return x, (sb_k, sb_v, fox_k, fox_v, fox_logf, h_last, conv_new)


def setup_inputs(seed: int = 0) -> dict:
    key = jax.random.key(seed)
    ks = jax.random.split(key, 32)
    f32 = jnp.float32

    def nrm(k, shape, s=1.0):
        return s * jax.random.normal(k, shape, f32)

    n_pages = PAST_LEN // PAGE_SIZE
    n_used = DEC_BATCH * n_pages
    n_pool = n_used + max(1, n_used // 4)
    perm = jax.random.permutation(ks[0], n_pool)
    page_table = perm[:n_used].reshape(DEC_BATCH, n_pages).astype(jnp.int32)

    sb_shape = (DEPTH, n_pool, PAGE_SIZE, SB_HEADS, HEAD_DIM)
    fox_shape = (DEPTH, n_pool, PAGE_SIZE, FOX_HEADS, HEAD_DIM)
    u_a = jax.random.uniform(ks[1], (DEPTH, LRU_WIDTH), f32, 0.9, 0.999)
    s_a = u_a ** (1.0 / LRU_C)
    lru_lambda = jnp.log(s_a) - jnp.log1p(-s_a)

    return {
        'x_prompt': nrm(ks[2], (BATCH, SEQ, D_MODEL)),
        'x_sample': nrm(ks[3], (DEC_BATCH, DEC_SEQ, D_MODEL)),
        'cache_sb_k': nrm(ks[4], sb_shape),
        'cache_sb_v': nrm(ks[5], sb_shape),
        'cache_fox_k': nrm(ks[6], fox_shape),
        'cache_fox_v': nrm(ks[7], fox_shape),
        'cache_fox_logf': jax.nn.log_sigmoid(FOX_BIAS_INIT + nrm(ks[8], (DEPTH, n_pool, PAGE_SIZE, FOX_HEADS))),
        'state_lru_h': nrm(ks[9], (DEPTH, DEC_BATCH, LRU_WIDTH), 0.5),
        'state_conv': nrm(ks[10], (DEPTH, DEC_BATCH, CONV_W - 1, LRU_WIDTH)),
        'page_table': page_table,
        'norm1_g': 1.0 + nrm(ks[11], (DEPTH, D_MODEL), 0.02),
        'w_in': nrm(ks[12], (DEPTH, D_MODEL, N_IN), D_MODEL ** -0.5),
        'b_gate': nrm(ks[13], (DEPTH, N_BRANCH * D_MODEL), 0.01),
        'conv_w': nrm(ks[14], (DEPTH, CONV_W, LRU_WIDTH), CONV_W ** -0.5),
        'conv_b': nrm(ks[15], (DEPTH, LRU_WIDTH), 0.01),
        'lru_wa': nrm(ks[16], (DEPTH, LRU_BLOCKS, LRU_BLOCK_W, LRU_BLOCK_W), LRU_BLOCK_W ** -0.5),
        'lru_ba': nrm(ks[17], (DEPTH, LRU_WIDTH), 0.01),
        'lru_wx': nrm(ks[18], (DEPTH, LRU_BLOCKS, LRU_BLOCK_W, LRU_BLOCK_W), LRU_BLOCK_W ** -0.5),
        'lru_bx': nrm(ks[19], (DEPTH, LRU_WIDTH), 0.01),
        'lru_lambda': lru_lambda,
        'fox_bf': FOX_BIAS_INIT + nrm(ks[20], (DEPTH, FOX_HEADS), 0.1),
        'w_br_lru': nrm(ks[21], (DEPTH, LRU_WIDTH, D_MODEL), LRU_WIDTH ** -0.5),
        'w_br_sb': nrm(ks[22], (DEPTH, SB_WIDTH, D_MODEL), SB_WIDTH ** -0.5),
        'w_br_fox': nrm(ks[23], (DEPTH, FOX_WIDTH, D_MODEL), FOX_WIDTH ** -0.5),
        'w_o': nrm(ks[24], (DEPTH, D_MODEL, D_MODEL), D_MODEL ** -0.5),
        'norm2_g': 1.0 + nrm(ks[25], (DEPTH, D_MODEL), 0.02),
        'w_up': nrm(ks[26], (DEPTH, D_MODEL, D_FF), D_MODEL ** -0.5),
        'w_down': nrm(ks[27], (DEPTH, D_FF, D_MODEL), D_FF ** -0.5),
        'final_g': 1.0 + nrm(ks[28], (D_MODEL,), 0.02),
    }


def reference(x_prompt, x_sample, cache_sb_k, cache_sb_v, cache_fox_k, cache_fox_v, cache_fox_logf,
              state_lru_h, state_conv, page_table, norm1_g, w_in, b_gate, conv_w, conv_b,
              lru_wa, lru_ba, lru_wx, lru_bx, lru_lambda, fox_bf, w_br_lru, w_br_sb, w_br_fox,
              w_o, norm2_g, w_up, w_down, final_g):
    f32 = jnp.float32
    xp, xs = x_prompt, x_sample
    st_p, st_s = [], []
    for l in range(DEPTH):
        w = (norm1_g[l], w_in[l], b_gate[l], conv_w[l], conv_b[l], lru_wa[l], lru_ba[l],
             lru_wx[l], lru_bx[l], lru_lambda[l], fox_bf[l], w_br_lru[l], w_br_sb[l],
             w_br_fox[l], w_o[l], norm2_g[l], w_up[l], w_down[l])
        empty_sb = jnp.zeros((BATCH, 0, SB_HEADS, HEAD_DIM), xp.dtype)
        empty_fox = jnp.zeros((BATCH, 0, FOX_HEADS, HEAD_DIM), xp.dtype)
        xp, sp = _layer(xp, empty_sb, empty_sb, empty_fox, empty_fox,
                        jnp.zeros((BATCH, 0, FOX_HEADS), f32),
                        jnp.zeros((BATCH, LRU_WIDTH), f32),
                        jnp.zeros((BATCH, CONV_W - 1, LRU_WIDTH), xp.dtype), *w)
        xs, ss = _layer(xs,
                        _gather_pages(cache_sb_k[l], page_table),
                        _gather_pages(cache_sb_v[l], page_table),
                        _gather_pages(cache_fox_k[l], page_table),
                        _gather_pages(cache_fox_v[l], page_table),
                        _gather_pages(cache_fox_logf[l], page_table),
                        state_lru_h[l], state_conv[l], *w)
        st_p.append(sp)
        st_s.append(ss)

    def stack(sts, i):
        return jnp.stack([s[i] for s in sts])

    y_prompt = _rms_norm(xp, final_g)
    y_sample = _rms_norm(xs, final_g)
    return (y_prompt, y_sample,
            stack(st_p, 0), stack(st_p, 1), stack(st_p, 2), stack(st_p, 3),
            stack(st_p, 4), stack(st_p, 5), stack(st_p, 6),
            stack(st_s, 0), stack(st_s, 1), stack(st_s, 2), stack(st_s, 3),
            stack(st_s, 4), stack(st_s, 5), stack(st_s, 6))
```

```python
import functools

import numpy as np
import jax
import jax.numpy as jnp
from jax import lax
from jax.experimental import pallas as pl
from jax.experimental.pallas import tpu as pltpu

F32 = jnp.float32
BF16 = jnp.bfloat16

RMS_EPS = 1e-6
LRU_C = 8.0
NEG_INF = -1e30
LANES = 128
SUBLANES = 8
SB_EXIT_LOG = -104.0
VMEM_LIMIT = 52 * 1024 * 1024


def _cparams(sem, vmem=VMEM_LIMIT):
    return pltpu.CompilerParams(dimension_semantics=sem, vmem_limit_bytes=vmem)


def _log_sigmoid(z):
    return jnp.minimum(z, 0.0) - jnp.log1p(jnp.exp(-jnp.abs(z)))


def _dot(a, b):
    return jnp.dot(a, b, preferred_element_type=F32)


def _dot_nt(a, b):
    return lax.dot_general(a, b, (((1,), (1,)), ((), ())), preferred_element_type=F32)


def _dot_exact_r(x, m01, parts):
    out = None
    r = x
    for p in range(parts):
        h = r.astype(BF16)
        d = _dot(h, m01)
        out = d if out is None else out + d
        if p + 1 < parts:
            r = r - h.astype(F32)
    return out


def _dot_exact_l(m01, x, parts):
    out = None
    r = x
    for p in range(parts):
        h = r.astype(BF16)
        d = _dot(m01, h)
        out = d if out is None else out + d
        if p + 1 < parts:
            r = r - h.astype(F32)
    return out


def _tri(n, kind):
    r = lax.broadcasted_iota(jnp.int32, (n, n), 0)
    c = lax.broadcasted_iota(jnp.int32, (n, n), 1)
    if kind == "row_gt_col":
        m = r > c
    elif kind == "row_le_col":
        m = r <= c
    else:
        raise ValueError(kind)
    return jnp.where(m, 1.0, 0.0).astype(BF16)


def _rmsnorm_kernel(x_ref, g_ref, o_ref):
    x = x_ref[...]
    ms = jnp.mean(x * x, axis=-1, keepdims=True)
    o_ref[...] = (x * lax.rsqrt(ms + RMS_EPS) * g_ref[...]).astype(o_ref.dtype)


def _rmsnorm(x, g, out_dtype):
    m, d = x.shape
    tm = min(m, 256)
    return pl.pallas_call(
        _rmsnorm_kernel,
        grid=(m // tm,),
        in_specs=[pl.BlockSpec((tm, d), lambda i: (i, 0)),
                  pl.BlockSpec((1, d), lambda i: (0, 0))],
        out_specs=pl.BlockSpec((tm, d), lambda i: (i, 0)),
        out_shape=jax.ShapeDtypeStruct((m, d), out_dtype),
        compiler_params=_cparams(("parallel",)),
        name="rmsnorm",
    )(x, g.reshape(1, d))


def _proj_kernel(x_ref, w_ref, b_ref, o_ref, wb_ref, *, act):
    @pl.when(pl.program_id(1) == 0)
    def _():
        wb_ref[...] = w_ref[...].astype(BF16)

    acc = _dot(x_ref[...], wb_ref[...])
    if act == "sigmoid":
        acc = jax.nn.sigmoid(acc + b_ref[...])
    elif act == "relu2":
        acc = jnp.square(jnp.maximum(acc, 0.0))
    o_ref[...] = acc.astype(o_ref.dtype)


def _proj(x, w3, layer, col_off, n, *, tn, act="none", bias=None, out_dtype=F32, tm=None, name="proj"):
    m, k = x.shape
    tm = tm or min(m, 512)
    assert col_off % tn == 0 and n % tn == 0 and m % tm == 0
    off = col_off // tn
    if bias is None:
        bias = jnp.zeros((1, n), F32)
    return pl.pallas_call(
        functools.partial(_proj_kernel, act=act),
        grid=(n // tn, m // tm),
        in_specs=[pl.BlockSpec((tm, k), lambda j, i: (i, 0)),
                  pl.BlockSpec((None, k, tn), lambda j, i: (layer, 0, off + j)),
                  pl.BlockSpec((1, tn), lambda j, i: (0, j))],
        out_specs=pl.BlockSpec((tm, tn), lambda j, i: (i, j)),
        out_shape=jax.ShapeDtypeStruct((m, n), out_dtype),
        scratch_shapes=[pltpu.VMEM((k, tn), BF16)],
        compiler_params=_cparams(("arbitrary", "arbitrary")),
        name=name,
    )(x, w3, bias)


def _resmm_kernel(x_ref, w_ref, r_ref, o_ref, wb_ref):
    @pl.when(pl.program_id(1) == 0)
    def _():
        wb_ref[...] = w_ref[...].astype(BF16)

    o_ref[...] = r_ref[...] + _dot(x_ref[...], wb_ref[...])


def _resmm(x, w3, layer, res, *, tn, tm=None, name="resmm"):
    m, k = x.shape
    n = w3.shape[2]
    tm = tm or min(m, 256)
    assert n % tn == 0 and m % tm == 0
    return pl.pallas_call(
        _resmm_kernel,
        grid=(n // tn, m // tm),
        in_specs=[pl.BlockSpec((tm, k), lambda j, i: (i, 0)),
                  pl.BlockSpec((None, k, tn), lambda j, i: (layer, 0, j), pipeline_mode=pl.Buffered(1)),
                  pl.BlockSpec((tm, tn), lambda j, i: (i, j))],
        out_specs=pl.BlockSpec((tm, tn), lambda j, i: (i, j)),
        out_shape=jax.ShapeDtypeStruct((m, n), F32),
        scratch_shapes=[pltpu.VMEM((k, tn), BF16)],
        compiler_params=_cparams(("arbitrary", "arbitrary")),
        name=name,
    )(x, w3, res)


def _merge_kernel(a0_ref, a1_ref, a2_ref, g0_ref, g1_ref, g2_ref, w0_ref, w1_ref, w2_ref, o_ref,
                  wb0_ref, wb1_ref, wb2_ref):
    @pl.when(pl.program_id(1) == 0)
    def _():
        wb0_ref[...] = w0_ref[...].astype(BF16)
        wb1_ref[...] = w1_ref[...].astype(BF16)
        wb2_ref[...] = w2_ref[...].astype(BF16)

    m = (g0_ref[...] * _dot(a0_ref[...], wb0_ref[...])
         + g1_ref[...] * _dot(a1_ref[...], wb1_ref[...])
         + g2_ref[...] * _dot(a2_ref[...], wb2_ref[...]))
    o_ref[...] = m.astype(o_ref.dtype)


def _merge(a0, a1, a2, gates, w0, w1, w2, layer, *, tn=512):
    m, k = a0.shape
    d = w0.shape[2]
    tm = min(m, 512)
    nb = d // tn
    a_spec = pl.BlockSpec((tm, k), lambda j, i: (i, 0))
    w_spec = pl.BlockSpec((None, k, tn), lambda j, i: (layer, 0, j))
    return pl.pallas_call(
        _merge_kernel,
        grid=(nb, m // tm),
        in_specs=[a_spec, a_spec, a_spec,
                  pl.BlockSpec((tm, tn), lambda j, i: (i, j)),
                  pl.BlockSpec((tm, tn), lambda j, i: (i, nb + j)),
                  pl.BlockSpec((tm, tn), lambda j, i: (i, 2 * nb + j)),
                  w_spec, w_spec, w_spec],
        out_specs=pl.BlockSpec((tm, tn), lambda j, i: (i, j)),
        out_shape=jax.ShapeDtypeStruct((m, d), BF16),
        scratch_shapes=[pltpu.VMEM((k, tn), BF16)] * 3,
        compiler_params=_cparams(("arbitrary", "arbitrary")),
        name="merge",
    )(a0, a1, a2, gates, gates, gates, w0, w1, w2)


def _forget_kernel(x_ref, wf_ref, bf_ref, lf_ref, cum_ref, carry_ref, *, tiles_per_seq, n_heads):
    i = pl.program_id(0)

    @pl.when(i % tiles_per_seq == 0)
    def _():
        carry_ref[...] = jnp.zeros_like(carry_ref)

    f = _dot_nt(wf_ref[...].astype(BF16), x_ref[...])[0:n_heads]
    lf = _log_sigmoid(f + bf_ref[...])
    lf_ref[...] = lf
    tm = lf.shape[1]
    cs = _dot_exact_r(lf, _tri(tm, "row_le_col"), 3) + carry_ref[...]
    cum_ref[...] = cs
    carry_ref[...] = cs[:, tm - 1:tm]


def _forget(xn, wf_t, bf, seq_len):
    m, k = xn.shape
    h = bf.shape[0]
    tm = min(seq_len, 256) if seq_len % 256 == 0 else m
    hp = wf_t.shape[0]
    return pl.pallas_call(
        functools.partial(_forget_kernel, tiles_per_seq=max(seq_len // tm, 1), n_heads=h),
        grid=(m // tm,),
        in_specs=[pl.BlockSpec((tm, k), lambda i: (i, 0)),
                  pl.BlockSpec((hp, k), lambda i: (0, 0)),
                  pl.BlockSpec((h, 1), lambda i: (0, 0))],
        out_specs=[pl.BlockSpec((h, tm), lambda i: (0, i)),
                   pl.BlockSpec((h, tm), lambda i: (0, i))],
        out_shape=[jax.ShapeDtypeStruct((h, m), F32), jax.ShapeDtypeStruct((h, m), F32)],
        scratch_shapes=[pltpu.VMEM((h, 1), F32)],
        compiler_params=_cparams(("arbitrary",)),
        name="forget",
    )(xn, wf_t, bf.reshape(h, 1))


def _lru_kernel(u_ref, h0_ref, c0_ref, cw_ref, cb_ref, wg_ref, ba_ref, bx_ref, lam_ref,
                y_ref, hl_ref, cn_ref, xbuf_ref, hc_ref, *, tt, r, n_grp, conv_w):
    j = pl.program_id(1)
    pad = SUBLANES

    @pl.when(j == 0)
    def _():
        xbuf_ref[0:pad, :] = jnp.zeros((pad, r), F32)
        xbuf_ref[pad - (conv_w - 1):pad, :] = c0_ref[...]
        hc_ref[...] = h0_ref[...]

    x = u_ref[:, 0:r]
    gate_in = u_ref[:, r:2 * r]
    xbuf_ref[pad:pad + tt, :] = x
    cw = cw_ref[...]
    xc = cb_ref[...] + x * cw[conv_w - 1:conv_w, :]
    for tap in range(conv_w - 1):
        shift = conv_w - 1 - tap
        xc = xc + xbuf_ref[pad - shift:pad - shift + tt, :] * cw[tap:tap + 1, :]

    gw = r // n_grp
    pre = []
    for c in range(n_grp):
        pre.append(_dot(xc[:, c * gw:(c + 1) * gw].astype(BF16), wg_ref[c].astype(BF16)))
    pa = jnp.concatenate([p[:, 0:gw] for p in pre], axis=1)
    px = jnp.concatenate([p[:, gw:2 * gw] for p in pre], axis=1)
    rg = jax.nn.sigmoid(pa + ba_ref[...])
    ig = jax.nn.sigmoid(px + bx_ref[...])
    log_a = LRU_C * rg * _log_sigmoid(lam_ref[...])
    a = jnp.exp(log_a)
    th = jnp.tanh(log_a)
    mult = jnp.sqrt(-2.0 * th / (1.0 - th))
    b = mult * (ig * xc)

    row = lax.broadcasted_iota(jnp.int32, (tt, r), 0)
    s = 1
    while s < tt:
        a_s = pltpu.roll(a, s, 0)
        b_s = pltpu.roll(b, s, 0)
        keep = row >= s
        b = jnp.where(keep, a * b_s + b, b)
        a = jnp.where(keep, a * a_s, a)
        s *= 2
    h = b + a * hc_ref[...]
    y_ref[...] = (h * jax.nn.gelu(gate_in)).astype(y_ref.dtype)
    hc_ref[...] = h[tt - 1:tt, :]
    xbuf_ref[0:pad, :] = xbuf_ref[tt:tt + pad, :]

    @pl.when(j == pl.num_programs(1) - 1)
    def _():
        hl_ref[...] = h[tt - 1:tt, :]
        cn_ref[...] = xbuf_ref[pad - (conv_w - 1):pad, :]


def _lru(u_lru, h0, conv0, conv_w, conv_b, wg, ba, bx, lam, n_seq, seq_len):
    m, r2 = u_lru.shape
    r = r2 // 2
    cwid = conv_w.shape[0]
    tt = min(seq_len, 256)
    nt = seq_len // tt
    n_grp = wg.shape[0]
    row = lambda a: a.reshape(1, r)
    y, hl, cn = pl.pallas_call(
        functools.partial(_lru_kernel, tt=tt, r=r, n_grp=n_grp, conv_w=cwid),
        grid=(n_seq, nt),
        in_specs=[pl.BlockSpec((tt, r2), lambda b, j: (b * nt + j, 0)),
                  pl.BlockSpec((None, 1, r), lambda b, j: (b, 0, 0)),
                  pl.BlockSpec((None, cwid - 1, r), lambda b, j: (b, 0, 0)),
                  pl.BlockSpec((cwid, r), lambda b, j: (0, 0)),
                  pl.BlockSpec((1, r), lambda b, j: (0, 0)),
                  pl.BlockSpec(wg.shape, lambda b, j: (0, 0, 0)),
                  pl.BlockSpec((1, r), lambda b, j: (0, 0)),
                  pl.BlockSpec((1, r), lambda b, j: (0, 0)),
                  pl.BlockSpec((1, r), lambda b, j: (0, 0))],
        out_specs=[pl.BlockSpec((tt, r), lambda b, j: (b * nt + j, 0)),
                   pl.BlockSpec((None, 1, r), lambda b, j: (b, 0, 0)),
                   pl.BlockSpec((None, cwid - 1, r), lambda b, j: (b, 0, 0))],
        out_shape=[jax.ShapeDtypeStruct((m, r), BF16 if tt % (2 * SUBLANES) == 0 else F32),
                   jax.ShapeDtypeStruct((n_seq, 1, r), F32),
                   jax.ShapeDtypeStruct((n_seq, cwid - 1, r), F32)],
        scratch_shapes=[pltpu.VMEM((tt + 2 * SUBLANES, r), F32), pltpu.VMEM((1, r), F32)],
        compiler_params=_cparams(("arbitrary", "arbitrary")),
        name="lru",
    )(u_lru, h0.reshape(n_seq, 1, r), conv0, conv_w, row(conv_b), wg, row(ba), row(bx), row(lam))
    return y, hl.reshape(n_seq, r), cn


def _cast_kv(k_ref, v_ref, kb_ref, vb_ref):
    kb_ref[...] = k_ref[...].astype(BF16)
    vb_ref[...] = v_ref[...].astype(BF16)


def _sb_prompt_kernel(q_ref, k_ref, v_ref, o_ref, kb_ref, vb_ref, *, blk, scale):
    qi = pl.program_id(2)

    @pl.when(qi == 0)
    def _():
        _cast_kv(k_ref, v_ref, kb_ref, vb_ref)

    q = (q_ref[...] * scale).astype(BF16)
    row = lax.broadcasted_iota(jnp.int32, (blk, blk), 0)
    col = lax.broadcasted_iota(jnp.int32, (blk, blk), 1)
    upper = _tri(blk, "row_gt_col")

    def block(kb, r_run, acc, masked):
        start = pl.multiple_of(kb * blk, blk)
        ks = kb_ref[pl.ds(start, blk), :]
        vs = vb_ref[pl.ds(start, blk), :]
        z = _dot_nt(q, ks)
        log_beta = _log_sigmoid(z)
        log_keep = log_beta - z
        if masked:
            before = col < row
            log_keep = jnp.where(before, log_keep, 0.0)
        later = _dot_exact_r(log_keep, upper, 2) + r_run
        w = jnp.exp(log_beta + later)
        if masked:
            w = jnp.where(before, w, 0.0)
        acc = acc + _dot(w.astype(BF16), vs)
        r_run = r_run + jnp.sum(log_keep, axis=1, keepdims=True)
        return r_run, acc

    r_run, acc = block(qi, jnp.zeros((blk, 1), F32), jnp.zeros((blk, LANES), F32), True)

    def cond(c):
        kb, r_c, _ = c
        return jnp.logical_and(kb >= 0, jnp.max(r_c) >= SB_EXIT_LOG)

    def body(c):
        kb, r_c, a_c = c
        r_c, a_c = block(kb, r_c, a_c, False)
        return kb - 1, r_c, a_c

    _, _, acc = lax.while_loop(cond, body, (qi - 1, r_run, acc))
    o_ref[...] = acc.astype(o_ref.dtype)


def _fox_prompt_kernel(q_ref, k_ref, v_ref, c_ref, o_ref, kb_ref, vb_ref, *, blk, scale):
    qi = pl.program_id(2)

    @pl.when(qi == 0)
    def _():
        _cast_kv(k_ref, v_ref, kb_ref, vb_ref)

    q = (q_ref[...] * scale).astype(BF16)
    row = lax.broadcasted_iota(jnp.int32, (blk, blk), 0)
    col = lax.broadcasted_iota(jnp.int32, (blk, blk), 1)
    c0 = c_ref[:, pl.ds(pl.multiple_of(qi * blk, blk), blk)][:, 0:1]

    def block(kb, m_run, l_run, acc, masked):
        start = pl.multiple_of(kb * blk, blk)
        ks = kb_ref[pl.ds(start, blk), :]
        vs = vb_ref[pl.ds(start, blk), :]
        s = _dot_nt(q, ks) + (c0 - c_ref[:, pl.ds(start, blk)])
        if masked:
            s = jnp.where(col <= row, s, NEG_INF)
        m_new = jnp.maximum(m_run, jnp.max(s, axis=1, keepdims=True))
        alpha = jnp.exp(m_run - m_new)
        p = jnp.exp(s - m_new)
        l_run = alpha * l_run + jnp.sum(p, axis=1, keepdims=True)
        acc = alpha * acc + _dot(p.astype(BF16), vs)
        return m_new, l_run, acc

    init = (jnp.full((blk, 1), NEG_INF, F32), jnp.zeros((blk, 1), F32), jnp.zeros((blk, LANES), F32))
    m_run, l_run, acc = lax.fori_loop(0, qi, lambda kb, c: block(kb, *c, False), init)
    m_run, l_run, acc = block(qi, m_run, l_run, acc, True)
    o_ref[...] = (acc / l_run).astype(o_ref.dtype)


def _prompt_attention(kernel, q, k, v, n_seq, seq_len, cum=None, name="attn"):
    m, width = q.shape
    h = width // LANES
    blk = min(seq_len, 256)
    nq = seq_len // blk
    scale = LANES ** -0.5
    kv_spec = pl.BlockSpec((seq_len, LANES), lambda b, hh, i: (b, hh))
    in_specs = [pl.BlockSpec((blk, LANES), lambda b, hh, i: (b * nq + i, hh)), kv_spec, kv_spec]
    args = [q, k, v]
    if cum is not None:
        in_specs.append(pl.BlockSpec((None, None, 1, seq_len), lambda b, hh, i: (b, hh, 0, 0)))
        args.append(cum)
    return pl.pallas_call(
        functools.partial(kernel, blk=blk, scale=scale),
        grid=(n_seq, h, nq),
        in_specs=in_specs,
        out_specs=pl.BlockSpec((blk, LANES), lambda b, hh, i: (b * nq + i, hh)),
        out_shape=jax.ShapeDtypeStruct((m, width), BF16),
        scratch_shapes=[pltpu.VMEM((seq_len, LANES), BF16), pltpu.VMEM((seq_len, LANES), BF16)],
        compiler_params=_cparams(("arbitrary", "arbitrary", "arbitrary")),
        name=name,
    )(*args)


def _head_scores(q, k_all, n_heads):
    return jnp.concatenate(
        [_dot_nt(q[:, h * LANES:(h + 1) * LANES], k_all[:, h * LANES:(h + 1) * LANES]) for h in range(n_heads)],
        axis=0)


def _head_outputs(w, v_all, n_heads, nq):
    return jnp.concatenate(
        [_dot(w[h * nq:(h + 1) * nq], v_all[:, h * LANES:(h + 1) * LANES]) for h in range(n_heads)], axis=1)


def _pad_rows(x, rows):
    return jnp.concatenate([x, jnp.zeros((rows - x.shape[0], x.shape[1]), x.dtype)], axis=0)


def _sb_sample_kernel(pt_ref, q_ref, kn_ref, vn_ref, kc_ref, vc_ref, o_ref, kbuf_ref, vbuf_ref, sem_ref,
                      *, layer, n_pages, page, n_heads, scale):
    b = pl.program_id(0)
    nq = q_ref.shape[0]
    rows = n_heads * nq
    q = q_ref[...] * scale
    qpos = lax.broadcasted_iota(jnp.int32, (rows, page), 0) % nq
    kidx = lax.broadcasted_iota(jnp.int32, (rows, page), 1)
    upper = _tri(page, "row_gt_col")

    def process(k_all, v_all, r_run, acc, masked):
        z = _head_scores(q, k_all, n_heads)
        log_beta = _log_sigmoid(z)
        log_keep = log_beta - z
        if masked:
            before = kidx < qpos
            log_keep = jnp.where(before, log_keep, 0.0)
        later = _dot_exact_r(log_keep, upper, 2) + r_run
        w = jnp.exp(log_beta + later)
        if masked:
            w = jnp.where(before, w, 0.0)
        acc = acc + _head_outputs(w, v_all, n_heads, nq)
        r_run = r_run + jnp.sum(log_keep, axis=1, keepdims=True)
        return r_run, acc

    r_run, acc = process(_pad_rows(kn_ref[...], page), _pad_rows(vn_ref[...], page),
                         jnp.zeros((rows, 1), F32), jnp.zeros((nq, n_heads * LANES), F32), True)

    def copies(p):
        pg = pt_ref[b, p]
        return (pltpu.make_async_copy(kc_ref.at[layer, pg], kbuf_ref, sem_ref.at[0]),
                pltpu.make_async_copy(vc_ref.at[layer, pg], vbuf_ref, sem_ref.at[1]))

    def cond(c):
        p, r_c, _ = c
        return jnp.logical_and(p >= 0, jnp.max(r_c) >= SB_EXIT_LOG)

    def body(c):
        p, r_c, a_c = c
        ck, cv = copies(p)
        ck.start()
        cv.start()
        ck.wait()
        cv.wait()
        r_c, a_c = process(kbuf_ref[...], vbuf_ref[...], r_c, a_c, False)
        return p - 1, r_c, a_c

    _, _, acc = lax.while_loop(cond, body, (n_pages - 1, r_run, acc))
    o_ref[...] = acc.astype(o_ref.dtype)


def _sb_sample(page_table, q, kn, vn, kcache, vcache, layer):
    n_req, n_pages = page_table.shape
    m, width = q.shape
    nq = m // n_req
    page = kcache.shape[2]
    h = width // LANES
    row_spec = pl.BlockSpec((nq, width), lambda b, pt: (b, 0))
    any_spec = pl.BlockSpec(memory_space=pl.ANY)
    grid_spec = pltpu.PrefetchScalarGridSpec(
        num_scalar_prefetch=1,
        grid=(n_req,),
        in_specs=[row_spec, row_spec, row_spec, any_spec, any_spec],
        out_specs=pl.BlockSpec((nq, width), lambda b, pt: (b, 0)),
        scratch_shapes=[pltpu.VMEM((page, width), F32), pltpu.VMEM((page, width), F32),
                        pltpu.SemaphoreType.DMA((2,))])
    return pl.pallas_call(
        functools.partial(_sb_sample_kernel, layer=layer, n_pages=n_pages, page=page, n_heads=h,
                          scale=LANES ** -0.5),
        grid_spec=grid_spec,
        out_shape=jax.ShapeDtypeStruct((m, width), F32),
        compiler_params=_cparams(("arbitrary",)),
        name="sb_sample",
    )(page_table, q, kn, vn, kcache, vcache)


def _cum_matrices(page, n_heads):
    w = page * n_heads
    src = np.arange(w)
    k_src, h_src = src // n_heads, src % n_heads
    h_dst, k_dst = src // page, src % page
    same = h_src[:, None] == h_dst[None, :]
    w_cum = same & (k_src[:, None] <= k_dst[None, :])
    w_tot = same
    head_mask = (h_src[None, :] == np.arange(SUBLANES)[:, None])
    return (jnp.asarray(w_cum, BF16), jnp.asarray(w_tot, BF16), jnp.asarray(head_mask, F32))


def _fox_cum_kernel(pt_ref, lfn_ref, wc_ref, wt_ref, hm_ref, lc_ref, cum_ref, cumn_ref, tot_ref,
                    xbuf_ref, sem_ref, *, layer, n_pages, page, n_heads):
    b = pl.program_id(0)

    def copy(p):
        return pltpu.make_async_copy(lc_ref.at[layer, pt_ref[b, p]], xbuf_ref.at[pl.ds(p, 1)], sem_ref.at[0])

    for p in range(n_pages):
        copy(p).start()
    for p in range(n_pages):
        copy(p).wait()

    x = xbuf_ref[...]
    within = _dot_exact_r(x, wc_ref[...], 3)
    totals = _dot_exact_r(x, wt_ref[...], 3)
    pr = lax.broadcasted_iota(jnp.int32, (n_pages, n_pages), 0)
    pc = lax.broadcasted_iota(jnp.int32, (n_pages, n_pages), 1)
    before = jnp.where(pc < pr, 1.0, 0.0).astype(BF16)
    carry = _dot_exact_l(before, totals, 3)
    cum = within + carry
    for h in range(n_heads):
        cum_ref[h] = cum[:, h * page:(h + 1) * page]
    colsum = _dot_exact_l(jnp.ones((SUBLANES, n_pages), BF16), x, 3)
    tot = _dot_exact_r(colsum * hm_ref[...], jnp.ones((page * n_heads, page), BF16), 3)[0:n_heads]
    tot_ref[...] = tot
    cumn_ref[...] = tot + _dot_exact_r(lfn_ref[...], _tri(page, "row_le_col"), 3)


def _fox_cum(page_table, lf_new_pad, lcache, layer):
    n_req, n_pages = page_table.shape
    _, h, page = lf_new_pad.shape
    w = page * h
    wc, wt, hm = _cum_matrices(page, h)
    const = lambda shape: pl.BlockSpec(shape, lambda b, pt: (0,) * len(shape))
    grid_spec = pltpu.PrefetchScalarGridSpec(
        num_scalar_prefetch=1,
        grid=(n_req,),
        in_specs=[pl.BlockSpec((None, h, page), lambda b, pt: (b, 0, 0)),
                  const((w, w)), const((w, w)), const((SUBLANES, w)),
                  pl.BlockSpec(memory_space=pl.ANY)],
        out_specs=[pl.BlockSpec((None, h, n_pages, page), lambda b, pt: (b, 0, 0, 0)),
                   pl.BlockSpec((None, h, page), lambda b, pt: (b, 0, 0)),
                   pl.BlockSpec((None, h, page), lambda b, pt: (b, 0, 0))],
        scratch_shapes=[pltpu.VMEM((n_pages, w), F32), pltpu.SemaphoreType.DMA((1,))])
    return pl.pallas_call(
        functools.partial(_fox_cum_kernel, layer=layer, n_pages=n_pages, page=page, n_heads=h),
        grid_spec=grid_spec,
        out_shape=[jax.ShapeDtypeStruct((n_req, h, n_pages, page), F32),
                   jax.ShapeDtypeStruct((n_req, h, page), F32),
                   jax.ShapeDtypeStruct((n_req, h, page), F32)],
        compiler_params=_cparams(("arbitrary",)),
        name="fox_cum",
    )(page_table, lf_new_pad, wc, wt, hm, lcache)


def _fox_sample_kernel(pt_ref, q_ref, kn_ref, vn_ref, kc_ref, vc_ref, cum_ref, cumn_ref, tot_ref, o_ref,
                       m_ref, l_ref, acc_ref, *, n_pages, page, n_heads, scale):
    p = pl.program_id(1)
    nq = q_ref.shape[0]
    rows = n_heads * nq
    q = q_ref[...] * scale

    @pl.when(p == 0)
    def _():
        m_ref[...] = jnp.full(m_ref.shape, NEG_INF, F32)
        l_ref[...] = jnp.zeros(l_ref.shape, F32)
        acc_ref[...] = jnp.zeros(acc_ref.shape, F32)

    def rows_of(per_head):
        return jnp.concatenate([jnp.broadcast_to(r, (nq, page)) for r in per_head], axis=0)

    shift = rows_of([tot_ref[h:h + 1, :] for h in range(n_heads)])

    def process(k_all, v_all, ck, masked):
        s = _head_scores(q, k_all, n_heads) + (shift - ck)
        if masked:
            qpos = lax.broadcasted_iota(jnp.int32, (rows, page), 0) % nq
            kidx = lax.broadcasted_iota(jnp.int32, (rows, page), 1)
            s = jnp.where(kidx <= qpos, s, NEG_INF)
        m_old = m_ref[...]
        m_new = jnp.maximum(m_old, jnp.max(s, axis=1, keepdims=True))
        alpha = jnp.exp(m_old - m_new)
        pr = jnp.exp(s - m_new)
        l_ref[...] = alpha * l_ref[...] + jnp.sum(pr, axis=1, keepdims=True)
        m_ref[...] = m_new
        pv = _head_outputs(pr, v_all, n_heads, nq)
        for h in range(n_heads):
            sl = slice(h * LANES, (h + 1) * LANES)
            acc_ref[:, sl] = alpha[h * nq:(h + 1) * nq] * acc_ref[:, sl] + pv[:, sl]

    @pl.when(p < n_pages)
    def _():
        ck = rows_of([cum_ref[h, pl.ds(p, 1), :] for h in range(n_heads)])
        process(kc_ref[...], vc_ref[...], ck, False)

    @pl.when(p == n_pages)
    def _():
        ck = rows_of([cumn_ref[h:h + 1, :] for h in range(n_heads)])
        process(_pad_rows(kn_ref[...], page), _pad_rows(vn_ref[...], page), ck, True)
        for h in range(n_heads):
            sl = slice(h * LANES, (h + 1) * LANES)
            o_ref[:, sl] = (acc_ref[:, sl] / l_ref[h * nq:(h + 1) * nq]).astype(o_ref.dtype)


def _fox_sample(page_table, q, kn, vn, kcache, vcache, cum_past, cum_new, tot, layer):
    n_req, n_pages = page_table.shape
    m, width = q.shape
    nq = m // n_req
    page = kcache.shape[2]
    h = width // LANES
    row_spec = pl.BlockSpec((nq, width), lambda b, p, pt: (b, 0))
    page_spec = pl.BlockSpec((None, None, page, width),
                             lambda b, p, pt: (layer, pt[b, jnp.minimum(p, n_pages - 1)], 0, 0))
    grid_spec = pltpu.PrefetchScalarGridSpec(
        num_scalar_prefetch=1,
        grid=(n_req, n_pages + 1),
        in_specs=[row_spec, row_spec, row_spec, page_spec, page_spec,
                  pl.BlockSpec((None, h, n_pages, page), lambda b, p, pt: (b, 0, 0, 0)),
                  pl.BlockSpec((None, h, page), lambda b, p, pt: (b, 0, 0)),
                  pl.BlockSpec((None, h, page), lambda b, p, pt: (b, 0, 0))],
        out_specs=pl.BlockSpec((nq, width), lambda b, p, pt: (b, 0)),
        scratch_shapes=[pltpu.VMEM((h * nq, 1), F32), pltpu.VMEM((h * nq, 1), F32),
                        pltpu.VMEM((nq, width), F32)])
    return pl.pallas_call(
        functools.partial(_fox_sample_kernel, n_pages=n_pages, page=page, n_heads=h, scale=LANES ** -0.5),
        grid_spec=grid_spec,
        out_shape=jax.ShapeDtypeStruct((m, width), F32),
        compiler_params=_cparams(("arbitrary", "arbitrary")),
        name="fox_sample",
    )(page_table, q, kn, vn, kcache, vcache, cum_past, cum_new, tot)


def _block_diag_gates(wa, wx, grp):
    g, w, _ = wa.shape
    eye = jnp.eye(grp, dtype=wa.dtype)

    def bd(x):
        x = x.reshape(g // grp, grp, w, w)
        return jnp.einsum("cgij,gh->cgihj", x, eye).reshape(g // grp, grp * w, grp * w)

    return jnp.concatenate([bd(wa), bd(wx)], axis=2)


def _layer_group(x, l, wts, n_seq, seq_len, h0, conv0, sample):
    (norm1_g, w_in, w_gate, b_gate, wf_t, fox_bf, conv_w, conv_b, wg, lru_ba, lru_bx, lru_lambda,
     w_br_lru, w_br_sb, w_br_fox, w_o, norm2_g, w_up, w_down) = wts
    m, d = x.shape
    r = lru_lambda.shape[1]
    width = w_br_sb.shape[1]
    h = width // LANES
    off_sb = 2 * r
    off_fox = off_sb + 3 * width

    xn = _rmsnorm(x, norm1_g[l], BF16)
    u_lru = _proj(xn, w_in, l, 0, 2 * r, tn=1024, name="proj_lru")
    sec = lambda off, nm: _proj(xn, w_in, l, off, width, tn=1024, name=nm)
    sb_q, sb_k, sb_v = sec(off_sb, "proj_sbq"), sec(off_sb + width, "proj_sbk"), sec(off_sb + 2 * width, "proj_sbv")
    fox_q, fox_k, fox_v = sec(off_fox, "proj_fq"), sec(off_fox + width, "proj_fk"), sec(off_fox + 2 * width, "proj_fv")
    gates = _proj(xn, w_gate, l, 0, 3 * d, tn=1024, act="sigmoid", bias=b_gate[l].reshape(1, 3 * d), name="proj_gate")
    lf_t, cum_t = _forget(xn, wf_t[l], fox_bf[l], seq_len)

    y_lru, h_last, conv_new = _lru(u_lru, h0, conv0, conv_w[l], conv_b[l], wg[l], lru_ba[l], lru_bx[l],
                                   lru_lambda[l], n_seq, seq_len)

    if sample is None:
        o_sb = _prompt_attention(_sb_prompt_kernel, sb_q, sb_k, sb_v, n_seq, seq_len, name="sb_prompt")
        cum = cum_t.reshape(h, n_seq, 1, seq_len).transpose(1, 0, 2, 3)
        o_fox = _prompt_attention(_fox_prompt_kernel, fox_q, fox_k, fox_v, n_seq, seq_len, cum=cum, name="fox_prompt")
    else:
        page_table, c_sb_k, c_sb_v, c_fox_k, c_fox_v, c_logf = sample
        page = c_sb_k.shape[2]
        o_sb = _sb_sample(page_table, sb_q, sb_k, sb_v, c_sb_k, c_sb_v, l)
        lf_new = lf_t.reshape(h, n_seq, seq_len).transpose(1, 0, 2)
        lf_new = jnp.pad(lf_new, ((0, 0), (0, 0), (0, page - seq_len)))
        cum_past, cum_new, tot = _fox_cum(page_table, lf_new, c_logf, l)
        o_fox = _fox_sample(page_table, fox_q, fox_k, fox_v, c_fox_k, c_fox_v, cum_past, cum_new, tot, l)

    merged = _merge(y_lru.astype(BF16), o_sb.astype(BF16), o_fox.astype(BF16), gates, w_br_lru, w_br_sb, w_br_fox, l)
    x = _resmm(merged, w_o, l, x, tn=1024, name="out_proj")
    hn = _rmsnorm(x, norm2_g[l], BF16)
    hid = _proj(hn, w_up, l, 0, w_up.shape[2], tn=1024, act="relu2", out_dtype=BF16, name="mlp_up")
    x = _resmm(hid, w_down, l, x, tn=512, name="mlp_down")

    fox_logf = lf_t.T.reshape(n_seq, seq_len, h)
    heads = lambda a: a.reshape(n_seq, seq_len, h, LANES)
    state = (heads(sb_k), heads(sb_v), heads(fox_k), heads(fox_v), fox_logf, h_last, conv_new)
    return x, state


def kernel(x_prompt, x_sample, cache_sb_k, cache_sb_v, cache_fox_k, cache_fox_v, cache_fox_logf, state_lru_h, state_conv, page_table, norm1_g, w_in, b_gate, conv_w, conv_b, lru_wa, lru_ba, lru_wx, lru_bx, lru_lambda, fox_bf, w_br_lru, w_br_sb, w_br_fox, w_o, norm2_g, w_up, w_down, final_g):
    depth = w_in.shape[0]
    n_p, t_p, d = x_prompt.shape
    n_s, t_s, _ = x_sample.shape
    r = lru_lambda.shape[1]
    n_pool, page, h_sb, hd = cache_sb_k.shape[1:]
    h_fox = cache_fox_k.shape[3]
    assert hd == LANES and h_sb == h_fox == SUBLANES
    width = h_sb * hd
    off_f = 2 * r + 6 * width
    off_gate = off_f + h_fox

    w_gate = w_in[:, :, off_gate:]
    wf_t = jnp.pad(jnp.swapaxes(w_in[:, :, off_f:off_gate], 1, 2), ((0, 0), (0, 2 * SUBLANES - h_fox), (0, 0)))
    wg = jax.vmap(lambda a, b: _block_diag_gates(a, b, 4))(lru_wa, lru_wx)
    wts = (norm1_g, w_in, w_gate, b_gate, wf_t, fox_bf, conv_w, conv_b, wg, lru_ba, lru_bx, lru_lambda,
           w_br_lru, w_br_sb, w_br_fox, w_o, norm2_g, w_up, w_down)

    flat = lambda c: c.reshape(depth, n_pool, page, -1)
    sample = (page_table, flat(cache_sb_k), flat(cache_sb_v), flat(cache_fox_k), flat(cache_fox_v),
              cache_fox_logf.reshape(depth, n_pool, 1, page * h_fox))

    xp = x_prompt.reshape(n_p * t_p, d)
    xs = x_sample.reshape(n_s * t_s, d)
    st_p, st_s = [], []
    for l in range(depth):
        xp, sp = _layer_group(xp, l, wts, n_p, t_p, jnp.zeros((n_p, r), F32),
                              jnp.zeros((n_p, conv_w.shape[1] - 1, r), F32), None)
        xs, ss = _layer_group(xs, l, wts, n_s, t_s, state_lru_h[l], state_conv[l], sample)
        st_p.append(sp)
        st_s.append(ss)

    y_prompt = _rmsnorm(xp, final_g, F32).reshape(n_p, t_p, d)
    y_sample = _rmsnorm(xs, final_g, F32).reshape(n_s, t_s, d)
    stack = lambda sts, i: jnp.stack([s[i] for s in sts])
    return (y_prompt, y_sample,
            *(stack(st_p, i) for i in range(7)),
            *(stack(st_s, i) for i in range(7)))
```

```python
import functools

import numpy as np
import jax
import jax.numpy as jnp
from jax import lax
from jax.experimental import pallas as pl
from jax.experimental.pallas import tpu as pltpu

F32 = jnp.float32
BF16 = jnp.bfloat16

RMS_EPS = 1e-6
LRU_C = 8.0
NEG_INF = -1e30
LANES = 128
SUBLANES = 8
SB_EXIT_LOG = -104.0
VMEM_LIMIT = 56 * 1024 * 1024


def _cparams(sem, vmem=VMEM_LIMIT):
    return pltpu.CompilerParams(dimension_semantics=sem, vmem_limit_bytes=vmem)


def _log_sigmoid(z):
    return jnp.minimum(z, 0.0) - jnp.log1p(jnp.exp(-jnp.abs(z)))


def _dot(a, b):
    return jnp.dot(a, b, preferred_element_type=F32)


def _dot_nt(a, b):
    return lax.dot_general(a, b, (((1,), (1,)), ((), ())), preferred_element_type=F32)


def _dot_exact_r(x, m01, parts):
    out = None
    r = x
    for p in range(parts):
        h = r.astype(BF16)
        d = _dot(h, m01)
        out = d if out is None else out + d
        if p + 1 < parts:
            r = r - h.astype(F32)
    return out


def _dot_exact_l(m01, x, parts):
    out = None
    r = x
    for p in range(parts):
        h = r.astype(BF16)
        d = _dot(m01, h)
        out = d if out is None else out + d
        if p + 1 < parts:
            r = r - h.astype(F32)
    return out


def _tri(n, kind):
    r = lax.broadcasted_iota(jnp.int32, (n, n), 0)
    c = lax.broadcasted_iota(jnp.int32, (n, n), 1)
    if kind == "row_gt_col":
        m = r > c
    elif kind == "row_le_col":
        m = r <= c
    else:
        raise ValueError(kind)
    return jnp.where(m, 1.0, 0.0).astype(BF16)


def _rmsnorm_kernel(x_ref, g_ref, o_ref):
    x = x_ref[...]
    ms = jnp.mean(x * x, axis=-1, keepdims=True)
    o_ref[...] = (x * lax.rsqrt(ms + RMS_EPS) * g_ref[...]).astype(o_ref.dtype)


def _rmsnorm(x, g, out_dtype):
    m, d = x.shape
    tm = min(m, 256)
    return pl.pallas_call(
        _rmsnorm_kernel,
        grid=(m // tm,),
        in_specs=[pl.BlockSpec((tm, d), lambda i: (i, 0)),
                  pl.BlockSpec((1, d), lambda i: (0, 0))],
        out_specs=pl.BlockSpec((tm, d), lambda i: (i, 0)),
        out_shape=jax.ShapeDtypeStruct((m, d), out_dtype),
        compiler_params=_cparams(("parallel",)),
        name="rmsnorm",
    )(x, g.reshape(1, d))


def _epi_plain(dots, extras, bias):
    return dots[0]


def _epi_sigmoid(dots, extras, bias):
    return jax.nn.sigmoid(dots[0] + bias)


def _epi_relu2(dots, extras, bias):
    return jnp.square(jnp.maximum(dots[0], 0.0))


def _epi_residual(dots, extras, bias):
    return extras[0] + dots[0]


def _epi_gated_sum(dots, extras, bias):
    out = extras[0] * dots[0]
    for e, d in zip(extras[1:], dots[1:]):
        out = out + e * d
    return out


def _gmm_kernel(*refs, n_act, n_extra, n_alias, steps, epilogue, fill_layer):
    n_grp = len(steps)
    pos = 0
    acts = [refs[pos + g * n_act: pos + (g + 1) * n_act] for g in range(n_grp)]
    pos += n_grp * n_act
    extras = [refs[pos + g * n_extra: pos + (g + 1) * n_extra] for g in range(n_grp)]
    pos += n_grp * n_extra
    w_refs = refs[pos:pos + n_act]
    pos += n_act
    b_ref = refs[pos]
    pos += 1 + n_alias
    o_refs = refs[pos:pos + n_grp]
    wb_refs = refs[pos + n_grp:]

    i = pl.program_id(1)

    @pl.when(i == 0)
    def _():
        for w, wb in zip(w_refs, wb_refs):
            wb[...] = w[...].astype(BF16)

    for g, (start, count) in enumerate(steps):
        @pl.when(jnp.logical_and(i >= start, i < start + count))
        def _(g=g):
            dots = [_dot(a[...], wb[...]) for a, wb in zip(acts[g], wb_refs)]
            out = epilogue(dots, [e[...] for e in extras[g]], b_ref[...]).astype(o_refs[g].dtype)
            if fill_layer is None:
                o_refs[g][...] = out
            else:
                for k in range(o_refs[g].shape[0]):
                    o_refs[g][k] = out if k == fill_layer else jnp.zeros_like(out)


def _gmm(groups, weights, layer, n, *, tn, epilogue, out_dtype, col_off=0, bias=None, extra_offs=(),
         single_buffer_w=False, stack=None, name="gmm"):
    assert col_off % tn == 0 and n % tn == 0
    off = col_off // tn
    n_act = len(weights)
    n_extra = len(extra_offs)
    steps, start = [], 0
    for acts, extras, tm in groups:
        m = acts[0].shape[0]
        assert m % tm == 0 and len(acts) == n_act and len(extras) == n_extra
        steps.append((start, m // tm))
        start += m // tm

    def row(g):
        s, c = steps[g]
        return lambda i: jnp.clip(i - s, 0, c - 1)

    in_specs, args = [], []
    for g, (acts, extras, tm) in enumerate(groups):
        for a in acts:
            in_specs.append(pl.BlockSpec((tm, a.shape[1]), lambda j, i, r=row(g): (r(i), 0)))
            args.append(a)
    for g, (acts, extras, tm) in enumerate(groups):
        for e, eo in zip(extras, extra_offs):
            in_specs.append(pl.BlockSpec((tm, tn), lambda j, i, r=row(g), eo=eo: (r(i), j + eo)))
            args.append(e)
    w_kw = dict(pipeline_mode=pl.Buffered(1)) if single_buffer_w else {}
    for w in weights:
        in_specs.append(pl.BlockSpec((None, w.shape[1], tn), lambda j, i: (layer, 0, off + j), **w_kw))
        args.append(w)
    if bias is None:
        bias = jnp.zeros((1, n), F32)
    in_specs.append(pl.BlockSpec((1, tn), lambda j, i: (0, j)))
    args.append(bias)

    aliases = {}
    n_alias = 0
    fill_layer = None
    if stack is not None:
        depth, prev = stack
        out_shape = [jax.ShapeDtypeStruct((depth, acts[0].shape[0], n), out_dtype) for acts, _, _ in groups]
        if prev is None:
            fill_layer = layer
            out_specs = [pl.BlockSpec((depth, tm, tn), lambda j, i, r=row(g): (0, r(i), j))
                         for g, (_, _, tm) in enumerate(groups)]
        else:
            out_specs = [pl.BlockSpec((None, tm, tn), lambda j, i, r=row(g): (layer, r(i), j))
                         for g, (_, _, tm) in enumerate(groups)]
            for g, p in enumerate(prev):
                aliases[len(args)] = g
                in_specs.append(pl.BlockSpec(memory_space=pl.ANY))
                args.append(p)
                n_alias += 1
    else:
        out_specs = [pl.BlockSpec((tm, tn), lambda j, i, r=row(g): (r(i), j))
                     for g, (_, _, tm) in enumerate(groups)]
        out_shape = [jax.ShapeDtypeStruct((acts[0].shape[0], n), out_dtype) for acts, _, _ in groups]

    return pl.pallas_call(
        functools.partial(_gmm_kernel, n_act=n_act, n_extra=n_extra, n_alias=n_alias, steps=tuple(steps),
                          epilogue=epilogue, fill_layer=fill_layer),
        grid=(n // tn, start),
        in_specs=in_specs,
        out_specs=out_specs,
        out_shape=out_shape,
        scratch_shapes=[pltpu.VMEM((w.shape[1], tn), BF16) for w in weights],
        input_output_aliases=aliases,
        compiler_params=_cparams(("arbitrary", "arbitrary")),
        name=name,
    )(*args)


def _forget_kernel(x_ref, wf_ref, bf_ref, lf_ref, cum_ref, carry_ref, *, tiles_per_seq, n_heads):
    i = pl.program_id(0)

    @pl.when(i % tiles_per_seq == 0)
    def _():
        carry_ref[...] = jnp.zeros_like(carry_ref)

    f = _dot_nt(wf_ref[...].astype(BF16), x_ref[...])[0:n_heads]
    lf = _log_sigmoid(f + bf_ref[...])
    lf_ref[...] = lf
    tm = lf.shape[1]
    cs = _dot_exact_r(lf, _tri(tm, "row_le_col"), 3) + carry_ref[...]
    cum_ref[...] = cs
    carry_ref[...] = cs[:, tm - 1:tm]


def _forget(xn, wf_t, bf, seq_len):
    m, k = xn.shape
    h = bf.shape[0]
    tm = min(seq_len, 256) if seq_len % 256 == 0 else m
    hp = wf_t.shape[0]
    return pl.pallas_call(
        functools.partial(_forget_kernel, tiles_per_seq=max(seq_len // tm, 1), n_heads=h),
        grid=(m // tm,),
        in_specs=[pl.BlockSpec((tm, k), lambda i: (i, 0)),
                  pl.BlockSpec((hp, k), lambda i: (0, 0)),
                  pl.BlockSpec((h, 1), lambda i: (0, 0))],
        out_specs=[pl.BlockSpec((h, tm), lambda i: (0, i)),
                   pl.BlockSpec((h, tm), lambda i: (0, i))],
        out_shape=[jax.ShapeDtypeStruct((h, m), F32), jax.ShapeDtypeStruct((h, m), F32)],
        scratch_shapes=[pltpu.VMEM((h, 1), F32)],
        compiler_params=_cparams(("arbitrary",)),
        name="forget",
    )(xn, wf_t, bf.reshape(h, 1))


def _lru_kernel(u_ref, h0_ref, c0_ref, cw_ref, cb_ref, wg_ref, ba_ref, bx_ref, lam_ref,
                y_ref, hl_ref, cn_ref, xbuf_ref, hc_ref, *, tt, r, n_grp, conv_w):
    j = pl.program_id(1)
    pad = SUBLANES

    @pl.when(j == 0)
    def _():
        xbuf_ref[0:pad, :] = jnp.zeros((pad, r), F32)
        xbuf_ref[pad - (conv_w - 1):pad, :] = c0_ref[...]
        hc_ref[...] = h0_ref[...]

    x = u_ref[:, 0:r]
    gate_in = u_ref[:, r:2 * r]
    xbuf_ref[pad:pad + tt, :] = x
    cw = cw_ref[...]
    xc = cb_ref[...] + x * cw[conv_w - 1:conv_w, :]
    for tap in range(conv_w - 1):
        shift = conv_w - 1 - tap
        xc = xc + xbuf_ref[pad - shift:pad - shift + tt, :] * cw[tap:tap + 1, :]

    gw = r // n_grp
    pre = []
    for c in range(n_grp):
        pre.append(_dot(xc[:, c * gw:(c + 1) * gw].astype(BF16), wg_ref[c].astype(BF16)))
    pa = jnp.concatenate([p[:, 0:gw] for p in pre], axis=1)
    px = jnp.concatenate([p[:, gw:2 * gw] for p in pre], axis=1)
    rg = jax.nn.sigmoid(pa + ba_ref[...])
    ig = jax.nn.sigmoid(px + bx_ref[...])
    log_a = LRU_C * rg * _log_sigmoid(lam_ref[...])
    a = jnp.exp(log_a)
    th = jnp.tanh(log_a)
    mult = jnp.sqrt(-2.0 * th / (1.0 - th))
    b = mult * (ig * xc)

    row = lax.broadcasted_iota(jnp.int32, (tt, r), 0)
    s = 1
    while s < tt:
        a_s = pltpu.roll(a, s, 0)
        b_s = pltpu.roll(b, s, 0)
        keep = row >= s
        b = jnp.where(keep, a * b_s + b, b)
        a = jnp.where(keep, a * a_s, a)
        s *= 2
    h = b + a * hc_ref[...]
    y_ref[...] = (h * jax.nn.gelu(gate_in)).astype(y_ref.dtype)
    hc_ref[...] = h[tt - 1:tt, :]
    xbuf_ref[0:pad, :] = xbuf_ref[tt:tt + pad, :]

    @pl.when(j == pl.num_programs(1) - 1)
    def _():
        hl_ref[...] = h[tt - 1:tt, :]
        cn_ref[...] = xbuf_ref[pad - (conv_w - 1):pad, :]


def _lru(u_lru, h0, conv0, conv_w, conv_b, wg, ba, bx, lam, n_seq, seq_len):
    m, r2 = u_lru.shape
    r = r2 // 2
    cwid = conv_w.shape[0]
    tt = min(seq_len, 256)
    nt = seq_len // tt
    n_grp = wg.shape[0]
    row = lambda a: a.reshape(1, r)
    y, hl, cn = pl.pallas_call(
        functools.partial(_lru_kernel, tt=tt, r=r, n_grp=n_grp, conv_w=cwid),
        grid=(n_seq, nt),
        in_specs=[pl.BlockSpec((tt, r2), lambda b, j: (b * nt + j, 0)),
                  pl.BlockSpec((None, 1, r), lambda b, j: (b, 0, 0)),
                  pl.BlockSpec((None, cwid - 1, r), lambda b, j: (b, 0, 0)),
                  pl.BlockSpec((cwid, r), lambda b, j: (0, 0)),
                  pl.BlockSpec((1, r), lambda b, j: (0, 0)),
                  pl.BlockSpec(wg.shape, lambda b, j: (0, 0, 0)),
                  pl.BlockSpec((1, r), lambda b, j: (0, 0)),
                  pl.BlockSpec((1, r), lambda b, j: (0, 0)),
                  pl.BlockSpec((1, r), lambda b, j: (0, 0))],
        out_specs=[pl.BlockSpec((tt, r), lambda b, j: (b * nt + j, 0)),
                   pl.BlockSpec((None, 1, r), lambda b, j: (b, 0, 0)),
                   pl.BlockSpec((None, cwid - 1, r), lambda b, j: (b, 0, 0))],
        out_shape=[jax.ShapeDtypeStruct((m, r), BF16 if tt % (2 * SUBLANES) == 0 else F32),
                   jax.ShapeDtypeStruct((n_seq, 1, r), F32),
                   jax.ShapeDtypeStruct((n_seq, cwid - 1, r), F32)],
        scratch_shapes=[pltpu.VMEM((tt + 2 * SUBLANES, r), F32), pltpu.VMEM((1, r), F32)],
        compiler_params=_cparams(("arbitrary", "arbitrary")),
        name="lru",
    )(u_lru, h0.reshape(n_seq, 1, r), conv0, conv_w, row(conv_b), wg, row(ba), row(bx), row(lam))
    return y, hl.reshape(n_seq, r), cn


def _cast_kv(k_ref, v_ref, kb_ref, vb_ref):
    kb_ref[...] = k_ref[...].astype(BF16)
    vb_ref[...] = v_ref[...].astype(BF16)


def _sb_prompt_kernel(q_ref, k_ref, v_ref, o_ref, kb_ref, vb_ref, *, bq, bk, hp, scale):
    qi = pl.program_id(2)
    ratio = bq // bk

    @pl.when(qi == 0)
    def _():
        _cast_kv(k_ref, v_ref, kb_ref, vb_ref)

    q_all = q_ref[...] * scale
    qs = [q_all[:, h * LANES:(h + 1) * LANES].astype(BF16) for h in range(hp)]
    row = lax.broadcasted_iota(jnp.int32, (bq, bk), 0)
    col = lax.broadcasted_iota(jnp.int32, (bq, bk), 1)
    upper = _tri(bk, "row_gt_col")

    def block(kb, carry, sub):
        start = pl.multiple_of(kb * bk, bk)
        out = []
        for h, (r_run, acc) in enumerate(carry):
            ks = kb_ref[pl.ds(start, bk), h * LANES:(h + 1) * LANES]
            vs = vb_ref[pl.ds(start, bk), h * LANES:(h + 1) * LANES]
            z = _dot_nt(qs[h], ks)
            log_beta = _log_sigmoid(z)
            log_keep = log_beta - z
            if sub is not None:
                before = col + sub * bk < row
                log_keep = jnp.where(before, log_keep, 0.0)
            later = _dot_exact_r(log_keep, upper, 2) + r_run
            w = jnp.exp(log_beta + later)
            if sub is not None:
                w = jnp.where(before, w, 0.0)
            acc = acc + _dot(w.astype(BF16), vs)
            r_run = r_run + jnp.sum(log_keep, axis=1, keepdims=True)
            out.append((r_run, acc))
        return tuple(out)

    carry = tuple((jnp.zeros((bq, 1), F32), jnp.zeros((bq, LANES), F32)) for _ in range(hp))
    for sub in reversed(range(ratio)):
        carry = block(qi * ratio + sub, carry, sub)

    def cond(c):
        kb, heads = c
        live = jnp.max(heads[0][0])
        for r_c, _ in heads[1:]:
            live = jnp.maximum(live, jnp.max(r_c))
        return jnp.logical_and(kb >= 0, live >= SB_EXIT_LOG)

    def body(c):
        kb, heads = c
        return kb - 1, block(kb, heads, None)

    _, carry = lax.while_loop(cond, body, (qi * ratio - 1, carry))
    for h, (_, acc) in enumerate(carry):
        o_ref[:, h * LANES:(h + 1) * LANES] = acc.astype(o_ref.dtype)


def _fox_prompt_kernel(q_ref, k_ref, v_ref, c_ref, o_ref, kb_ref, vb_ref, *, bq, bk, hp, scale):
    qi = pl.program_id(2)
    ratio = bq // bk

    @pl.when(qi == 0)
    def _():
        _cast_kv(k_ref, v_ref, kb_ref, vb_ref)

    q_all = q_ref[...] * scale
    qs = [q_all[:, h * LANES:(h + 1) * LANES].astype(BF16) for h in range(hp)]
    row = lax.broadcasted_iota(jnp.int32, (bq, bk), 0)
    col = lax.broadcasted_iota(jnp.int32, (bq, bk), 1)
    c0 = [c_ref[h, :, pl.ds(pl.multiple_of(qi * bq, bq), bk)][:, 0:1] for h in range(hp)]

    def block(kb, carry, sub):
        start = pl.multiple_of(kb * bk, bk)
        out = []
        for h, (m_run, l_run, acc) in enumerate(carry):
            ks = kb_ref[pl.ds(start, bk), h * LANES:(h + 1) * LANES]
            vs = vb_ref[pl.ds(start, bk), h * LANES:(h + 1) * LANES]
            s = _dot_nt(qs[h], ks) + (c0[h] - c_ref[h, :, pl.ds(start, bk)])
            if sub is not None:
                s = jnp.where(col + sub * bk <= row, s, NEG_INF)
            m_new = jnp.maximum(m_run, jnp.max(s, axis=1, keepdims=True))
            alpha = jnp.exp(m_run - m_new)
            p = jnp.exp(s - m_new)
            l_run = alpha * l_run + jnp.sum(p, axis=1, keepdims=True)
            acc = alpha * acc + _dot(p.astype(BF16), vs)
            out.append((m_new, l_run, acc))
        return tuple(out)

    carry = tuple((jnp.full((bq, 1), NEG_INF, F32), jnp.zeros((bq, 1), F32), jnp.zeros((bq, LANES), F32))
                  for _ in range(hp))
    carry = lax.fori_loop(0, qi * ratio, lambda kb, c: block(kb, c, None), carry)
    for sub in range(ratio):
        carry = block(qi * ratio + sub, carry, sub)
    for h, (_, l_run, acc) in enumerate(carry):
        o_ref[:, h * LANES:(h + 1) * LANES] = (acc / l_run).astype(o_ref.dtype)


def _prompt_attention(kernel, q, k3, v3, layer, n_seq, seq_len, cum=None, name="attn"):
    m, width = q.shape
    hp = 2
    h = width // (hp * LANES)
    bq = min(seq_len, 512)
    bk = min(seq_len, 256)
    nq = seq_len // bq
    scale = LANES ** -0.5
    kv_spec = pl.BlockSpec((None, seq_len, hp * LANES), lambda b, hh, i: (layer, b, hh))
    in_specs = [pl.BlockSpec((bq, hp * LANES), lambda b, hh, i: (b * nq + i, hh)), kv_spec, kv_spec]
    args = [q, k3, v3]
    if cum is not None:
        in_specs.append(pl.BlockSpec((None, hp, 1, seq_len), lambda b, hh, i: (b, hh, 0, 0)))
        args.append(cum)
    return pl.pallas_call(
        functools.partial(kernel, bq=bq, bk=bk, hp=hp, scale=scale),
        grid=(n_seq, h, nq),
        in_specs=in_specs,
        out_specs=pl.BlockSpec((bq, hp * LANES), lambda b, hh, i: (b * nq + i, hh)),
        out_shape=jax.ShapeDtypeStruct((m, width), BF16),
        scratch_shapes=[pltpu.VMEM((seq_len, hp * LANES), BF16), pltpu.VMEM((seq_len, hp * LANES), BF16)],
        compiler_params=_cparams(("arbitrary", "arbitrary", "arbitrary")),
        name=name,
    )(*args)


def _pad_rows(x, rows):
    return jnp.concatenate([x, jnp.zeros((rows - x.shape[0], x.shape[1]), x.dtype)], axis=0)


def _sb_sample_kernel(pt_ref, q_ref, kn_ref, vn_ref, kc_ref, vc_ref, o_ref, kbuf_ref, vbuf_ref, sem_ref,
                      *, layer, n_pages, page, n_heads, scale):
    b = pl.program_id(0)
    nq = q_ref.shape[0]
    rows = n_heads * nq
    q = q_ref[...] * scale
    qh = [q[:, h * LANES:(h + 1) * LANES] for h in range(n_heads)]
    qpos = lax.broadcasted_iota(jnp.int32, (rows, page), 0) % nq
    kidx = lax.broadcasted_iota(jnp.int32, (rows, page), 1)
    upper = _tri(page, "row_gt_col")

    def process(k_heads, v_heads, r_run, acc, masked):
        z = jnp.concatenate([_dot_nt(qh[h], k_heads[h]) for h in range(n_heads)], axis=0)
        log_beta = _log_sigmoid(z)
        log_keep = log_beta - z
        if masked:
            before = kidx < qpos
            log_keep = jnp.where(before, log_keep, 0.0)
        later = _dot_exact_r(log_keep, upper, 2) + r_run
        w = jnp.exp(log_beta + later)
        if masked:
            w = jnp.where(before, w, 0.0)
        acc = acc + jnp.concatenate([_dot(w[h * nq:(h + 1) * nq], v_heads[h]) for h in range(n_heads)], axis=1)
        r_run = r_run + jnp.sum(log_keep, axis=1, keepdims=True)
        return r_run, acc

    kn, vn = kn_ref[...], vn_ref[...]
    r_run, acc = process([_pad_rows(kn[:, h * LANES:(h + 1) * LANES], page) for h in range(n_heads)],
                         [_pad_rows(vn[:, h * LANES:(h + 1) * LANES], page) for h in range(n_heads)],
                         jnp.zeros((rows, 1), F32), jnp.zeros((nq, n_heads * LANES), F32), True)

    def copies(p):
        pg = pt_ref[b, p]
        return (pltpu.make_async_copy(kc_ref.at[layer, pg], kbuf_ref, sem_ref.at[0]),
                pltpu.make_async_copy(vc_ref.at[layer, pg], vbuf_ref, sem_ref.at[1]))

    def cond(c):
        p, r_c, _ = c
        return jnp.logical_and(p >= 0, jnp.max(r_c) >= SB_EXIT_LOG)

    def body(c):
        p, r_c, a_c = c
        ck, cv = copies(p)
        ck.start()
        cv.start()
        ck.wait()
        cv.wait()
        r_c, a_c = process([kbuf_ref[pl.ds(h, page, stride=n_heads), :] for h in range(n_heads)],
                           [vbuf_ref[pl.ds(h, page, stride=n_heads), :] for h in range(n_heads)],
                           r_c, a_c, False)
        return p - 1, r_c, a_c

    _, _, acc = lax.while_loop(cond, body, (n_pages - 1, r_run, acc))
    o_ref[...] = acc.astype(o_ref.dtype)


def _sb_sample(page_table, q, kn3, vn3, kcache, vcache, layer, n_heads):
    n_req, n_pages = page_table.shape
    m, width = q.shape
    nq = m // n_req
    page = kcache.shape[2] // n_heads
    new_spec = pl.BlockSpec((None, nq, width), lambda b, pt: (layer, b, 0))
    any_spec = pl.BlockSpec(memory_space=pl.ANY)
    grid_spec = pltpu.PrefetchScalarGridSpec(
        num_scalar_prefetch=1,
        grid=(n_req,),
        in_specs=[pl.BlockSpec((nq, width), lambda b, pt: (b, 0)), new_spec, new_spec, any_spec, any_spec],
        out_specs=pl.BlockSpec((nq, width), lambda b, pt: (b, 0)),
        scratch_shapes=[pltpu.VMEM((page * n_heads, LANES), F32), pltpu.VMEM((page * n_heads, LANES), F32),
                        pltpu.SemaphoreType.DMA((2,))])
    return pl.pallas_call(
        functools.partial(_sb_sample_kernel, layer=layer, n_pages=n_pages, page=page, n_heads=n_heads,
                          scale=LANES ** -0.5),
        grid_spec=grid_spec,
        out_shape=jax.ShapeDtypeStruct((m, width), F32),
        compiler_params=_cparams(("arbitrary",)),
        name="sb_sample",
    )(page_table, q, kn3, vn3, kcache, vcache)


def _cum_matrices(page, n_heads):
    idx = np.arange(page * n_heads)
    key, head = idx // n_heads, idx % n_heads
    same = head[:, None] == head[None, :]
    w_cum = same & (key[:, None] <= key[None, :])
    return jnp.asarray(w_cum, BF16), jnp.asarray(same, BF16)


def _fox_cum_kernel(pt_ref, lfn_ref, wc_ref, wt_ref, lc_ref, cum_ref, cumn_ref, tot_ref, xbuf_ref, sem_ref,
                    *, layer, n_pages):
    b = pl.program_id(0)

    def copy(p):
        return pltpu.make_async_copy(lc_ref.at[layer, pt_ref[b, p]], xbuf_ref.at[pl.ds(p, 1)], sem_ref.at[0])

    for p in range(n_pages):
        copy(p).start()
    for p in range(n_pages):
        copy(p).wait()

    x = xbuf_ref[...]
    within = _dot_exact_r(x, wc_ref[...], 3)
    totals = _dot_exact_r(x, wt_ref[...], 3)
    pr = lax.broadcasted_iota(jnp.int32, (n_pages, n_pages), 0)
    pc = lax.broadcasted_iota(jnp.int32, (n_pages, n_pages), 1)
    earlier = jnp.where(pc < pr, 1.0, 0.0).astype(BF16)
    carry = _dot_exact_l(earlier, totals, 3)
    cum_ref[...] = within + carry
    tot = carry[n_pages - 1:n_pages] + totals[n_pages - 1:n_pages]
    tot_ref[...] = tot
    cumn_ref[...] = tot + _dot_exact_r(lfn_ref[...], wc_ref[...], 3)[0:1]


def _fox_cum(page_table, lf_new, lcache, layer, page, n_heads):
    n_req, n_pages = page_table.shape
    w = page * n_heads
    wc, wt = _cum_matrices(page, n_heads)
    const = lambda shape: pl.BlockSpec(shape, lambda b, pt: (0,) * len(shape))
    grid_spec = pltpu.PrefetchScalarGridSpec(
        num_scalar_prefetch=1,
        grid=(n_req,),
        in_specs=[pl.BlockSpec((None, lf_new.shape[1], w), lambda b, pt: (b, 0, 0)),
                  const((w, w)), const((w, w)),
                  pl.BlockSpec(memory_space=pl.ANY)],
        out_specs=[pl.BlockSpec((None, n_pages, w), lambda b, pt: (b, 0, 0)),
                   pl.BlockSpec((None, 1, w), lambda b, pt: (b, 0, 0)),
                   pl.BlockSpec((None, 1, w), lambda b, pt: (b, 0, 0))],
        scratch_shapes=[pltpu.VMEM((n_pages, w), F32), pltpu.SemaphoreType.DMA((1,))])
    return pl.pallas_call(
        functools.partial(_fox_cum_kernel, layer=layer, n_pages=n_pages),
        grid_spec=grid_spec,
        out_shape=[jax.ShapeDtypeStruct((n_req, n_pages, w), F32),
                   jax.ShapeDtypeStruct((n_req, 1, w), F32),
                   jax.ShapeDtypeStruct((n_req, 1, w), F32)],
        compiler_params=_cparams(("arbitrary",)),
        name="fox_cum",
    )(page_table, lf_new, wc, wt, lcache)


def _fox_sample_kernel(pt_ref, q_ref, kn_ref, vn_ref, *refs, n_pages, grp, n_heads, scale):
    k_refs, v_refs = refs[:grp], refs[grp:2 * grp]
    cum_ref, cumn_ref, tot_ref, o_ref, m_ref, l_ref, acc_ref = refs[2 * grp:]
    p = pl.program_id(1)
    n_steps = n_pages // grp
    nq = q_ref.shape[0]
    rows = n_heads * nq
    w = k_refs[0].shape[0]
    q = q_ref[...] * scale
    qall = jnp.concatenate([q[:, h * LANES:(h + 1) * LANES] for h in range(n_heads)], axis=0)
    lane = lax.broadcasted_iota(jnp.int32, (rows, w), 1)
    rowi = lax.broadcasted_iota(jnp.int32, (rows, w), 0)
    own_head = (lane % n_heads) == (rowi // nq)
    tot = tot_ref[...]

    @pl.when(p == 0)
    def _():
        m_ref[...] = jnp.full(m_ref.shape, NEG_INF, F32)
        l_ref[...] = jnp.zeros(l_ref.shape, F32)
        acc_ref[...] = jnp.zeros(acc_ref.shape, F32)

    def update(scores, values):
        m_old = m_ref[...]
        m_new = m_old
        for s in scores:
            m_new = jnp.maximum(m_new, jnp.max(s, axis=1, keepdims=True))
        alpha = jnp.exp(m_old - m_new)
        l_new = alpha * l_ref[...]
        pv = None
        for s, v in zip(scores, values):
            pr = jnp.exp(s - m_new)
            l_new = l_new + jnp.sum(pr, axis=1, keepdims=True)
            d = _dot(pr, v)
            pv = d if pv is None else pv + d
        m_ref[...] = m_new
        l_ref[...] = l_new
        acc_ref[...] = alpha * acc_ref[...] + pv

    @pl.when(p < n_steps)
    def _():
        scores = []
        for g in range(grp):
            ck = cum_ref[pl.ds(p * grp + g, 1), :]
            s = _dot_nt(qall, k_refs[g][...]) + (tot - ck)
            scores.append(jnp.where(own_head, s, NEG_INF))
        update(scores, [v_refs[g][...] for g in range(grp)])

    @pl.when(p == n_steps)
    def _():
        s = _dot_nt(qall, _pad_rows(kn_ref[...], w)) + (tot - cumn_ref[...])
        causal = jnp.logical_and(own_head, (lane // n_heads) <= (rowi % nq))
        update([jnp.where(causal, s, NEG_INF)], [_pad_rows(vn_ref[...], w)])
        out = acc_ref[...] / l_ref[...]
        for h in range(n_heads):
            o_ref[:, h * LANES:(h + 1) * LANES] = out[h * nq:(h + 1) * nq].astype(o_ref.dtype)


def _fox_sample(page_table, q, kn2, vn2, kcache, vcache, cum_past, cum_new, tot, layer, n_heads):
    n_req, n_pages = page_table.shape
    m, width = q.shape
    nq = m // n_req
    w = kcache.shape[2]
    grp = 8 if n_pages % 8 == 0 else (4 if n_pages % 4 == 0 else 1)
    n_steps = n_pages // grp

    def page_spec(g):
        return pl.BlockSpec((None, None, w, LANES),
                            lambda b, p, pt: (layer, pt[b, jnp.minimum(p, n_steps - 1) * grp + g], 0, 0))

    new_spec = pl.BlockSpec((nq * n_heads, LANES), lambda b, p, pt: (b, 0))
    grid_spec = pltpu.PrefetchScalarGridSpec(
        num_scalar_prefetch=1,
        grid=(n_req, n_steps + 1),
        in_specs=[pl.BlockSpec((nq, width), lambda b, p, pt: (b, 0)), new_spec, new_spec]
                 + [page_spec(g) for g in range(grp)] + [page_spec(g) for g in range(grp)]
                 + [pl.BlockSpec((None, n_pages, w), lambda b, p, pt: (b, 0, 0)),
                    pl.BlockSpec((None, 1, w), lambda b, p, pt: (b, 0, 0)),
                    pl.BlockSpec((None, 1, w), lambda b, p, pt: (b, 0, 0))],
        out_specs=pl.BlockSpec((nq, width), lambda b, p, pt: (b, 0)),
        scratch_shapes=[pltpu.VMEM((n_heads * nq, 1), F32), pltpu.VMEM((n_heads * nq, 1), F32),
                        pltpu.VMEM((n_heads * nq, LANES), F32)])
    return pl.pallas_call(
        functools.partial(_fox_sample_kernel, n_pages=n_pages, grp=grp, n_heads=n_heads, scale=LANES ** -0.5),
        grid_spec=grid_spec,
        out_shape=jax.ShapeDtypeStruct((m, width), F32),
        compiler_params=_cparams(("arbitrary", "arbitrary")),
        name="fox_sample",
    )(page_table, q, kn2, vn2, *([kcache] * grp), *([vcache] * grp), cum_past, cum_new, tot)


def _block_diag_gates(wa, wx, grp):
    g, w, _ = wa.shape
    eye = jnp.eye(grp, dtype=wa.dtype)

    def bd(x):
        x = x.reshape(g // grp, grp, w, w)
        return jnp.einsum("cgij,gh->cgihj", x, eye).reshape(g // grp, grp * w, grp * w)

    return jnp.concatenate([bd(wa), bd(wx)], axis=2)


def _layer(xs, l, wts, dims, lru_state, sample, prev_kv):
    (norm1_g, w_in, w_gate, b_gate, wf_t, fox_bf, conv_w, conv_b, wg, lru_ba, lru_bx, lru_lambda,
     w_br_lru, w_br_sb, w_br_fox, w_o, norm2_g, w_up, w_down) = wts
    depth = w_in.shape[0]
    d = xs[0].shape[1]
    r = lru_lambda.shape[1]
    width = w_br_sb.shape[1]
    h = width // LANES
    off_sb = 2 * r
    off_fox = off_sb + 3 * width
    tms = [min(x.shape[0], 1024) for x in xs]

    xn = [_rmsnorm(x, norm1_g[l], BF16) for x in xs]
    grp = lambda acts, extras=None: [([a[g] for a in acts], [e[g] for e in (extras or [])], tms[g])
                                     for g in range(len(xs))]
    proj = lambda off, n, nm, **kw: _gmm(grp([xn]), [w_in], l, n, tn=1024, epilogue=_epi_plain, out_dtype=F32,
                                         col_off=off, name=nm, **kw)
    u_lru = proj(0, 2 * r, "proj_lru")
    sb_q = proj(off_sb, width, "proj_sbq")
    fox_q = proj(off_fox, width, "proj_fq")
    kv = []
    for idx, (off, nm) in enumerate([(off_sb + width, "proj_sbk"), (off_sb + 2 * width, "proj_sbv"),
                                     (off_fox + width, "proj_fk"), (off_fox + 2 * width, "proj_fv")]):
        kv.append(_gmm([([xn[g]], [], min(tms[g], 512)) for g in range(len(xs))], [w_in], l, width, tn=1024,
                       epilogue=_epi_plain, out_dtype=F32, col_off=off, name=nm,
                       stack=(depth, None if prev_kv is None else prev_kv[idx])))
    gates = _gmm(grp([xn]), [w_gate], l, 3 * d, tn=1024, epilogue=_epi_sigmoid, out_dtype=F32,
                 bias=b_gate[l].reshape(1, 3 * d), name="proj_gate")

    branches, states = [], []
    for g, (n_seq, seq_len) in enumerate(dims):
        lf_t, cum_t = _forget(xn[g], wf_t[l], fox_bf[l], seq_len)
        h0, conv0 = lru_state[g]
        y_lru, h_last, conv_new = _lru(u_lru[g], h0, conv0, conv_w[l], conv_b[l], wg[l], lru_ba[l], lru_bx[l],
                                       lru_lambda[l], n_seq, seq_len)
        lf = lf_t.T
        if g == 0:
            o_sb = _prompt_attention(_sb_prompt_kernel, sb_q[g], kv[0][g], kv[1][g], l, n_seq, seq_len,
                                     name="sb_prompt")
            cum = cum_t.reshape(h, n_seq, 1, seq_len).transpose(1, 0, 2, 3)
            o_fox = _prompt_attention(_fox_prompt_kernel, fox_q[g], kv[2][g], kv[3][g], l, n_seq, seq_len,
                                      cum=cum, name="fox_prompt")
        else:
            page_table, c_sb_k, c_sb_v, c_fox_k, c_fox_v, c_logf, page = sample
            o_sb = _sb_sample(page_table, sb_q[g], kv[0][g], kv[1][g], c_sb_k, c_sb_v, l, h)
            lf_new = jnp.pad(lf.reshape(n_seq, 1, seq_len * h), ((0, 0), (0, 15), (0, (page - seq_len) * h)))
            cum_past, cum_new, tot = _fox_cum(page_table, lf_new, c_logf, l, page, h)
            rows = lambda a: a[l].reshape(n_seq * seq_len * h, LANES)
            o_fox = _fox_sample(page_table, fox_q[g], rows(kv[2][g]), rows(kv[3][g]), c_fox_k, c_fox_v,
                                cum_past, cum_new, tot, l, h)
        branches.append((y_lru.astype(BF16), o_sb.astype(BF16), o_fox.astype(BF16)))
        states.append((lf.reshape(n_seq, seq_len, h), h_last, conv_new))

    acts = [[b[i] for b in branches] for i in range(3)]
    nb = d // 512
    tms_mid = [min(t, 512) for t in tms]
    merged = _gmm([([a[g] for a in acts], [gates[g]] * 3, tms_mid[g]) for g in range(len(xs))],
                  [w_br_lru, w_br_sb, w_br_fox], l, d, tn=512, epilogue=_epi_gated_sum,
                  out_dtype=BF16, extra_offs=(0, nb, 2 * nb), name="merge")
    x1 = _gmm([([merged[g]], [xs[g]], tms_mid[g]) for g in range(len(xs))], [w_o], l, d, tn=1024,
              epilogue=_epi_residual, out_dtype=F32, extra_offs=(0,), name="out_proj")
    hn = [_rmsnorm(x, norm2_g[l], BF16) for x in x1]
    hid = _gmm(grp([hn]), [w_up], l, w_up.shape[2], tn=1024, epilogue=_epi_relu2, out_dtype=BF16, name="mlp_up")
    tms_down = [min(t, 256) for t in tms]
    x2 = _gmm([([hid[g]], [x1[g]], tms_down[g]) for g in range(len(xs))], [w_down], l, d, tn=512,
              epilogue=_epi_residual, out_dtype=F32, extra_offs=(0,), single_buffer_w=True, name="mlp_down")
    return x2, kv, states


def kernel(x_prompt, x_sample, cache_sb_k, cache_sb_v, cache_fox_k, cache_fox_v, cache_fox_logf, state_lru_h, state_conv, page_table, norm1_g, w_in, b_gate, conv_w, conv_b, lru_wa, lru_ba, lru_wx, lru_bx, lru_lambda, fox_bf, w_br_lru, w_br_sb, w_br_fox, w_o, norm2_g, w_up, w_down, final_g):
    depth = w_in.shape[0]
    n_p, t_p, d = x_prompt.shape
    n_s, t_s, _ = x_sample.shape
    r = lru_lambda.shape[1]
    n_pool, page, h_sb, hd = cache_sb_k.shape[1:]
    h_fox = cache_fox_k.shape[3]
    assert hd == LANES and h_sb == h_fox == SUBLANES
    width = h_sb * hd
    off_f = 2 * r + 6 * width
    off_gate = off_f + h_fox

    w_gate = w_in[:, :, off_gate:]
    wf_t = jnp.pad(jnp.swapaxes(w_in[:, :, off_f:off_gate], 1, 2), ((0, 0), (0, 2 * SUBLANES - h_fox), (0, 0)))
    wg = jax.vmap(lambda a, b: _block_diag_gates(a, b, 4))(lru_wa, lru_wx)
    wts = (norm1_g, w_in, w_gate, b_gate, wf_t, fox_bf, conv_w, conv_b, wg, lru_ba, lru_bx, lru_lambda,
           w_br_lru, w_br_sb, w_br_fox, w_o, norm2_g, w_up, w_down)

    rows = lambda c: c.reshape(depth, n_pool, page * h_sb, hd)
    sample = (page_table, rows(cache_sb_k), rows(cache_sb_v), rows(cache_fox_k), rows(cache_fox_v),
              cache_fox_logf.reshape(depth, n_pool, 1, page * h_fox), page)

    xs = [x_prompt.reshape(n_p * t_p, d), x_sample.reshape(n_s * t_s, d)]
    dims = [(n_p, t_p), (n_s, t_s)]
    kv = None
    small = []
    for l in range(depth):
        lru_state = [(jnp.zeros((n_p, r), F32), jnp.zeros((n_p, conv_w.shape[1] - 1, r), F32)),
                     (state_lru_h[l], state_conv[l])]
        xs, kv, states = _layer(xs, l, wts, dims, lru_state, sample, kv)
        small.append(states)

    y = [_rmsnorm(x, final_g, F32).reshape(n, t, d) for x, (n, t) in zip(xs, dims)]
    outs = []
    for g, (n, t) in enumerate(dims):
        outs.extend(a[g].reshape(depth, n, t, h_sb, hd) for a in kv)
        outs.extend(jnp.stack([small[l][g][i] for l in range(depth)]) for i in range(3))
    return (y[0], y[1], *outs)
```

```python
import functools

import numpy as np
import jax
import jax.numpy as jnp
from jax import lax
from jax.experimental import pallas as pl
from jax.experimental.pallas import tpu as pltpu

F32 = jnp.float32
BF16 = jnp.bfloat16

RMS_EPS = 1e-6
LRU_C = 8.0
NEG_INF = -1e30
LANES = 128
SUBLANES = 8
SB_EXIT_LOG = -104.0
VMEM_LIMIT = 56 * 1024 * 1024


def _cparams(sem, vmem=VMEM_LIMIT):
    return pltpu.CompilerParams(dimension_semantics=sem, vmem_limit_bytes=vmem)


def _log_sigmoid(z):
    return jnp.minimum(z, 0.0) - jnp.log1p(jnp.exp(-jnp.abs(z)))


def _dot(a, b):
    return jnp.dot(a, b, preferred_element_type=F32)


def _dot_nt(a, b):
    return lax.dot_general(a, b, (((1,), (1,)), ((), ())), preferred_element_type=F32)


def _dot_exact_r(x, m01, parts):
    out = None
    r = x
    for p in range(parts):
        h = r.astype(BF16)
        d = _dot(h, m01)
        out = d if out is None else out + d
        if p + 1 < parts:
            r = r - h.astype(F32)
    return out


def _dot_exact_l(m01, x, parts):
    out = None
    r = x
    for p in range(parts):
        h = r.astype(BF16)
        d = _dot(m01, h)
        out = d if out is None else out + d
        if p + 1 < parts:
            r = r - h.astype(F32)
    return out


def _tri(n, kind):
    r = lax.broadcasted_iota(jnp.int32, (n, n), 0)
    c = lax.broadcasted_iota(jnp.int32, (n, n), 1)
    if kind == "row_gt_col":
        m = r > c
    elif kind == "row_le_col":
        m = r <= c
    else:
        raise ValueError(kind)
    return jnp.where(m, 1.0, 0.0).astype(BF16)


def _rmsnorm_kernel(x_ref, g_ref, o_ref):
    x = x_ref[...]
    ms = jnp.mean(x * x, axis=-1, keepdims=True)
    o_ref[...] = (x * lax.rsqrt(ms + RMS_EPS) * g_ref[...]).astype(o_ref.dtype)


def _rmsnorm(x, g, out_dtype):
    m, d = x.shape
    tm = min(m, 256)
    return pl.pallas_call(
        _rmsnorm_kernel,
        grid=(m // tm,),
        in_specs=[pl.BlockSpec((tm, d), lambda i: (i, 0)),
                  pl.BlockSpec((1, d), lambda i: (0, 0))],
        out_specs=pl.BlockSpec((tm, d), lambda i: (i, 0)),
        out_shape=jax.ShapeDtypeStruct((m, d), out_dtype),
        compiler_params=_cparams(("parallel",)),
        name="rmsnorm",
    )(x, g.reshape(1, d))


def _epi_plain(dots, extras, bias):
    return dots[0]


def _epi_sigmoid(dots, extras, bias):
    return jax.nn.sigmoid(dots[0] + bias)


def _epi_relu2(dots, extras, bias):
    return jnp.square(jnp.maximum(dots[0], 0.0))


def _epi_residual(dots, extras, bias):
    return extras[0] + dots[0]


def _epi_gated_sum(dots, extras, bias):
    out = extras[0] * dots[0]
    for e, d in zip(extras[1:], dots[1:]):
        out = out + e * d
    return out


def _gmm_kernel(*refs, n_act, n_extra, n_alias, steps, epilogue, fill_layer):
    n_grp = len(steps)
    pos = 0
    acts = [refs[pos + g * n_act: pos + (g + 1) * n_act] for g in range(n_grp)]
    pos += n_grp * n_act
    extras = [refs[pos + g * n_extra: pos + (g + 1) * n_extra] for g in range(n_grp)]
    pos += n_grp * n_extra
    w_refs = refs[pos:pos + n_act]
    pos += n_act
    b_ref = refs[pos]
    pos += 1 + n_alias
    o_refs = refs[pos:pos + n_grp]
    wb_refs = refs[pos + n_grp:]

    i = pl.program_id(1)

    @pl.when(i == 0)
    def _():
        for w, wb in zip(w_refs, wb_refs):
            wb[...] = w[...].astype(BF16)

    for g, (start, count) in enumerate(steps):
        @pl.when(jnp.logical_and(i >= start, i < start + count))
        def _(g=g):
            dots = [_dot(a[...], wb[...]) for a, wb in zip(acts[g], wb_refs)]
            out = epilogue(dots, [e[...] for e in extras[g]], b_ref[...]).astype(o_refs[g].dtype)
            if fill_layer is None:
                o_refs[g][...] = out
            else:
                for k in range(o_refs[g].shape[0]):
                    o_refs[g][k] = out if k == fill_layer else jnp.zeros_like(out)


def _gmm(groups, weights, layer, n, *, tn, epilogue, out_dtype, col_off=0, bias=None, extra_offs=(),
         single_buffer_w=False, stack=None, name="gmm"):
    assert col_off % tn == 0 and n % tn == 0
    off = col_off // tn
    n_act = len(weights)
    n_extra = len(extra_offs)
    steps, start = [], 0
    for acts, extras, tm in groups:
        m = acts[0].shape[0]
        assert m % tm == 0 and len(acts) == n_act and len(extras) == n_extra
        steps.append((start, m // tm))
        start += m // tm

    def row(g):
        s, c = steps[g]
        return lambda i: jnp.clip(i - s, 0, c - 1)

    in_specs, args = [], []
    for g, (acts, extras, tm) in enumerate(groups):
        for a in acts:
            in_specs.append(pl.BlockSpec((tm, a.shape[1]), lambda j, i, r=row(g): (r(i), 0)))
            args.append(a)
    for g, (acts, extras, tm) in enumerate(groups):
        for e, eo in zip(extras, extra_offs):
            in_specs.append(pl.BlockSpec((tm, tn), lambda j, i, r=row(g), eo=eo: (r(i), j + eo)))
            args.append(e)
    w_kw = dict(pipeline_mode=pl.Buffered(1)) if single_buffer_w else {}
    for w in weights:
        in_specs.append(pl.BlockSpec((None, w.shape[1], tn), lambda j, i: (layer, 0, off + j), **w_kw))
        args.append(w)
    if bias is None:
        bias = jnp.zeros((1, n), F32)
    in_specs.append(pl.BlockSpec((1, tn), lambda j, i: (0, j)))
    args.append(bias)

    aliases = {}
    n_alias = 0
    fill_layer = None
    if stack is not None:
        depth, prev = stack
        out_shape = [jax.ShapeDtypeStruct((depth, acts[0].shape[0], n), out_dtype) for acts, _, _ in groups]
        if prev is None:
            fill_layer = layer
            out_specs = [pl.BlockSpec((depth, tm, tn), lambda j, i, r=row(g): (0, r(i), j))
                         for g, (_, _, tm) in enumerate(groups)]
        else:
            out_specs = [pl.BlockSpec((None, tm, tn), lambda j, i, r=row(g): (layer, r(i), j))
                         for g, (_, _, tm) in enumerate(groups)]
            for g, p in enumerate(prev):
                aliases[len(args)] = g
                in_specs.append(pl.BlockSpec(memory_space=pl.ANY))
                args.append(p)
                n_alias += 1
    else:
        out_specs = [pl.BlockSpec((tm, tn), lambda j, i, r=row(g): (r(i), j))
                     for g, (_, _, tm) in enumerate(groups)]
        out_shape = [jax.ShapeDtypeStruct((acts[0].shape[0], n), out_dtype) for acts, _, _ in groups]

    return pl.pallas_call(
        functools.partial(_gmm_kernel, n_act=n_act, n_extra=n_extra, n_alias=n_alias, steps=tuple(steps),
                          epilogue=epilogue, fill_layer=fill_layer),
        grid=(n // tn, start),
        in_specs=in_specs,
        out_specs=out_specs,
        out_shape=out_shape,
        scratch_shapes=[pltpu.VMEM((w.shape[1], tn), BF16) for w in weights],
        input_output_aliases=aliases,
        compiler_params=_cparams(("arbitrary", "arbitrary")),
        name=name,
    )(*args)


def _forget_kernel(x_ref, wf_ref, bf_ref, lf_ref, cum_ref, carry_ref, *, tiles_per_seq, n_heads):
    i = pl.program_id(0)

    @pl.when(i % tiles_per_seq == 0)
    def _():
        carry_ref[...] = jnp.zeros_like(carry_ref)

    f = _dot_nt(wf_ref[...].astype(BF16), x_ref[...])[0:n_heads]
    lf = _log_sigmoid(f + bf_ref[...])
    lf_ref[...] = lf
    tm = lf.shape[1]
    cs = _dot_exact_r(lf, _tri(tm, "row_le_col"), 3) + carry_ref[...]
    cum_ref[...] = cs
    carry_ref[...] = cs[:, tm - 1:tm]


def _forget(xn, wf_t, bf, seq_len):
    m, k = xn.shape
    h = bf.shape[0]
    tm = min(seq_len, 256) if seq_len % 256 == 0 else m
    hp = wf_t.shape[0]
    return pl.pallas_call(
        functools.partial(_forget_kernel, tiles_per_seq=max(seq_len // tm, 1), n_heads=h),
        grid=(m // tm,),
        in_specs=[pl.BlockSpec((tm, k), lambda i: (i, 0)),
                  pl.BlockSpec((hp, k), lambda i: (0, 0)),
                  pl.BlockSpec((h, 1), lambda i: (0, 0))],
        out_specs=[pl.BlockSpec((h, tm), lambda i: (0, i)),
                   pl.BlockSpec((h, tm), lambda i: (0, i))],
        out_shape=[jax.ShapeDtypeStruct((h, m), F32), jax.ShapeDtypeStruct((h, m), F32)],
        scratch_shapes=[pltpu.VMEM((h, 1), F32)],
        compiler_params=_cparams(("arbitrary",)),
        name="forget",
    )(xn, wf_t, bf.reshape(h, 1))


def _lru_kernel(u_ref, h0_ref, c0_ref, cw_ref, cb_ref, wg_ref, ba_ref, bx_ref, lam_ref,
                y_ref, hl_ref, cn_ref, xbuf_ref, hc_ref, *, tt, r, n_grp, conv_w):
    j = pl.program_id(1)
    pad = SUBLANES

    @pl.when(j == 0)
    def _():
        xbuf_ref[0:pad, :] = jnp.zeros((pad, r), F32)
        xbuf_ref[pad - (conv_w - 1):pad, :] = c0_ref[...]
        hc_ref[...] = h0_ref[...]

    x = u_ref[:, 0:r]
    gate_in = u_ref[:, r:2 * r]
    xbuf_ref[pad:pad + tt, :] = x
    cw = cw_ref[...]
    xc = cb_ref[...] + x * cw[conv_w - 1:conv_w, :]
    for tap in range(conv_w - 1):
        shift = conv_w - 1 - tap
        xc = xc + xbuf_ref[pad - shift:pad - shift + tt, :] * cw[tap:tap + 1, :]

    gw = r // n_grp
    pre = []
    for c in range(n_grp):
        pre.append(_dot(xc[:, c * gw:(c + 1) * gw].astype(BF16), wg_ref[c].astype(BF16)))
    pa = jnp.concatenate([p[:, 0:gw] for p in pre], axis=1)
    px = jnp.concatenate([p[:, gw:2 * gw] for p in pre], axis=1)
    rg = jax.nn.sigmoid(pa + ba_ref[...])
    ig = jax.nn.sigmoid(px + bx_ref[...])
    log_a = LRU_C * rg * _log_sigmoid(lam_ref[...])
    a = jnp.exp(log_a)
    th = jnp.tanh(log_a)
    mult = jnp.sqrt(-2.0 * th / (1.0 - th))
    b = mult * (ig * xc)

    in_grp = lax.broadcasted_iota(jnp.int32, (tt, r), 0) % SUBLANES
    s = 1
    while s < SUBLANES:
        a_s = pltpu.roll(a, s, 0)
        b_s = pltpu.roll(b, s, 0)
        keep = in_grp >= s
        b = jnp.where(keep, a * b_s + b, b)
        a = jnp.where(keep, a * a_s, a)
        s *= 2
    h_prev = hc_ref[...]
    h_grps = []
    for g in range(tt // SUBLANES):
        rows = slice(g * SUBLANES, (g + 1) * SUBLANES)
        h_g = b[rows] + a[rows] * h_prev
        h_grps.append(h_g)
        h_prev = h_g[SUBLANES - 1:SUBLANES]
    h = jnp.concatenate(h_grps, axis=0)
    y_ref[...] = (h * jax.nn.gelu(gate_in)).astype(y_ref.dtype)
    hc_ref[...] = h[tt - 1:tt, :]
    xbuf_ref[0:pad, :] = xbuf_ref[tt:tt + pad, :]

    @pl.when(j == pl.num_programs(1) - 1)
    def _():
        hl_ref[...] = h[tt - 1:tt, :]
        cn_ref[...] = xbuf_ref[pad - (conv_w - 1):pad, :]


def _lru(u_lru, h0, conv0, conv_w, conv_b, wg, ba, bx, lam, n_seq, seq_len):
    m, r2 = u_lru.shape
    r = r2 // 2
    cwid = conv_w.shape[0]
    tt = min(seq_len, 256)
    nt = seq_len // tt
    n_grp = wg.shape[0]
    row = lambda a: a.reshape(1, r)
    y, hl, cn = pl.pallas_call(
        functools.partial(_lru_kernel, tt=tt, r=r, n_grp=n_grp, conv_w=cwid),
        grid=(n_seq, nt),
        in_specs=[pl.BlockSpec((tt, r2), lambda b, j: (b * nt + j, 0)),
                  pl.BlockSpec((None, 1, r), lambda b, j: (b, 0, 0)),
                  pl.BlockSpec((None, cwid - 1, r), lambda b, j: (b, 0, 0)),
                  pl.BlockSpec((cwid, r), lambda b, j: (0, 0)),
                  pl.BlockSpec((1, r), lambda b, j: (0, 0)),
                  pl.BlockSpec(wg.shape, lambda b, j: (0, 0, 0)),
                  pl.BlockSpec((1, r), lambda b, j: (0, 0)),
                  pl.BlockSpec((1, r), lambda b, j: (0, 0)),
                  pl.BlockSpec((1, r), lambda b, j: (0, 0))],
        out_specs=[pl.BlockSpec((tt, r), lambda b, j: (b * nt + j, 0)),
                   pl.BlockSpec((None, 1, r), lambda b, j: (b, 0, 0)),
                   pl.BlockSpec((None, cwid - 1, r), lambda b, j: (b, 0, 0))],
        out_shape=[jax.ShapeDtypeStruct((m, r), BF16 if tt % (2 * SUBLANES) == 0 else F32),
                   jax.ShapeDtypeStruct((n_seq, 1, r), F32),
                   jax.ShapeDtypeStruct((n_seq, cwid - 1, r), F32)],
        scratch_shapes=[pltpu.VMEM((tt + 2 * SUBLANES, r), F32), pltpu.VMEM((1, r), F32)],
        compiler_params=_cparams(("arbitrary", "arbitrary")),
        name="lru",
    )(u_lru, h0.reshape(n_seq, 1, r), conv0, conv_w, row(conv_b), wg, row(ba), row(bx), row(lam))
    return y, hl.reshape(n_seq, r), cn


def _cast_kv(k_ref, v_ref, kb_ref, vb_ref):
    kb_ref[...] = k_ref[...].astype(BF16)
    vb_ref[...] = v_ref[...].astype(BF16)


def _sb_prompt_kernel(q_ref, k_ref, v_ref, o_ref, kb_ref, vb_ref, *, bq, bk, hp, scale):
    qi = pl.program_id(2)
    ratio = bq // bk

    @pl.when(qi == 0)
    def _():
        _cast_kv(k_ref, v_ref, kb_ref, vb_ref)

    q_all = q_ref[...] * scale
    qs = [q_all[:, h * LANES:(h + 1) * LANES].astype(BF16) for h in range(hp)]
    row = lax.broadcasted_iota(jnp.int32, (bq, bk), 0)
    col = lax.broadcasted_iota(jnp.int32, (bq, bk), 1)
    upper = _tri(bk, "row_gt_col")

    def block(kb, carry, sub):
        start = pl.multiple_of(kb * bk, bk)
        out = []
        for h, (r_run, acc) in enumerate(carry):
            ks = kb_ref[pl.ds(start, bk), h * LANES:(h + 1) * LANES]
            vs = vb_ref[pl.ds(start, bk), h * LANES:(h + 1) * LANES]
            z = _dot_nt(qs[h], ks)
            log_beta = _log_sigmoid(z)
            log_keep = log_beta - z
            if sub is not None:
                before = col + sub * bk < row
                log_keep = jnp.where(before, log_keep, 0.0)
            later = _dot_exact_r(log_keep, upper, 2) + r_run
            w = jnp.exp(log_beta + later)
            if sub is not None:
                w = jnp.where(before, w, 0.0)
            acc = acc + _dot(w.astype(BF16), vs)
            r_run = r_run + jnp.sum(log_keep, axis=1, keepdims=True)
            out.append((r_run, acc))
        return tuple(out)

    carry = tuple((jnp.zeros((bq, 1), F32), jnp.zeros((bq, LANES), F32)) for _ in range(hp))
    for sub in reversed(range(ratio)):
        carry = block(qi * ratio + sub, carry, sub)

    def cond(c):
        kb, heads = c
        live = jnp.max(heads[0][0])
        for r_c, _ in heads[1:]:
            live = jnp.maximum(live, jnp.max(r_c))
        return jnp.logical_and(kb >= 0, live >= SB_EXIT_LOG)

    def body(c):
        kb, heads = c
        return kb - 1, block(kb, heads, None)

    _, carry = lax.while_loop(cond, body, (qi * ratio - 1, carry))
    for h, (_, acc) in enumerate(carry):
        o_ref[:, h * LANES:(h + 1) * LANES] = acc.astype(o_ref.dtype)


def _fox_prompt_kernel(q_ref, k_ref, v_ref, c_ref, o_ref, kb_ref, vb_ref, *, bq, bk, hp, scale):
    qi = pl.program_id(2)
    ratio = bq // bk

    @pl.when(qi == 0)
    def _():
        _cast_kv(k_ref, v_ref, kb_ref, vb_ref)

    q_all = q_ref[...] * scale
    qs = [q_all[:, h * LANES:(h + 1) * LANES].astype(BF16) for h in range(hp)]
    row = lax.broadcasted_iota(jnp.int32, (bq, bk), 0)
    col = lax.broadcasted_iota(jnp.int32, (bq, bk), 1)
    c0 = [c_ref[h, :, pl.ds(pl.multiple_of(qi * bq, bq), bk)][:, 0:1] for h in range(hp)]

    def block(kb, carry, sub):
        start = pl.multiple_of(kb * bk, bk)
        out = []
        for h, (m_run, l_run, acc) in enumerate(carry):
            ks = kb_ref[pl.ds(start, bk), h * LANES:(h + 1) * LANES]
            vs = vb_ref[pl.ds(start, bk), h * LANES:(h + 1) * LANES]
            s = _dot_nt(qs[h], ks) + (c0[h] - c_ref[h, :, pl.ds(start, bk)])
            if sub is not None:
                s = jnp.where(col + sub * bk <= row, s, NEG_INF)
            m_new = jnp.maximum(m_run, jnp.max(s, axis=1, keepdims=True))
            alpha = jnp.exp(m_run - m_new)
            p = jnp.exp(s - m_new)
            l_run = alpha * l_run + jnp.sum(p, axis=1, keepdims=True)
            acc = alpha * acc + _dot(p.astype(BF16), vs)
            out.append((m_new, l_run, acc))
        return tuple(out)

    carry = tuple((jnp.full((bq, 1), NEG_INF, F32), jnp.zeros((bq, 1), F32), jnp.zeros((bq, LANES), F32))
                  for _ in range(hp))
    carry = lax.fori_loop(0, qi * ratio, lambda kb, c: block(kb, c, None), carry)
    for sub in range(ratio):
        carry = block(qi * ratio + sub, carry, sub)
    for h, (_, l_run, acc) in enumerate(carry):
        o_ref[:, h * LANES:(h + 1) * LANES] = (acc / l_run).astype(o_ref.dtype)


def _prompt_attention(kernel, q, k3, v3, layer, n_seq, seq_len, cum=None, bk_max=256, name="attn"):
    m, width = q.shape
    hp = 2
    h = width // (hp * LANES)
    bq = min(seq_len, 512)
    bk = min(seq_len, bk_max)
    nq = seq_len // bq
    scale = LANES ** -0.5
    kv_spec = pl.BlockSpec((None, seq_len, hp * LANES), lambda b, hh, i: (layer, b, hh))
    in_specs = [pl.BlockSpec((bq, hp * LANES), lambda b, hh, i: (b * nq + i, hh)), kv_spec, kv_spec]
    args = [q, k3, v3]
    if cum is not None:
        in_specs.append(pl.BlockSpec((None, hp, 1, seq_len), lambda b, hh, i: (b, hh, 0, 0)))
        args.append(cum)
    return pl.pallas_call(
        functools.partial(kernel, bq=bq, bk=bk, hp=hp, scale=scale),
        grid=(n_seq, h, nq),
        in_specs=in_specs,
        out_specs=pl.BlockSpec((bq, hp * LANES), lambda b, hh, i: (b * nq + i, hh)),
        out_shape=jax.ShapeDtypeStruct((m, width), BF16),
        scratch_shapes=[pltpu.VMEM((seq_len, hp * LANES), BF16), pltpu.VMEM((seq_len, hp * LANES), BF16)],
        compiler_params=_cparams(("arbitrary", "arbitrary", "arbitrary")),
        name=name,
    )(*args)


def _pad_rows(x, rows):
    return jnp.concatenate([x, jnp.zeros((rows - x.shape[0], x.shape[1]), x.dtype)], axis=0)


def _sb_sample_kernel(pt_ref, q_ref, kn_ref, vn_ref, kc_ref, vc_ref, o_ref, kbuf_ref, vbuf_ref, sem_ref,
                      *, layer, n_pages, page, n_heads, scale):
    b = pl.program_id(0)
    nq = q_ref.shape[0]
    rows = n_heads * nq
    q = q_ref[...] * scale
    qh = [q[:, h * LANES:(h + 1) * LANES] for h in range(n_heads)]
    qpos = lax.broadcasted_iota(jnp.int32, (rows, page), 0) % nq
    kidx = lax.broadcasted_iota(jnp.int32, (rows, page), 1)
    upper = _tri(page, "row_gt_col")

    def process(k_heads, v_heads, r_run, acc, masked):
        z = jnp.concatenate([_dot_nt(qh[h], k_heads[h]) for h in range(n_heads)], axis=0)
        log_beta = _log_sigmoid(z)
        log_keep = log_beta - z
        if masked:
            before = kidx < qpos
            log_keep = jnp.where(before, log_keep, 0.0)
        later = _dot_exact_r(log_keep, upper, 2) + r_run
        w = jnp.exp(log_beta + later)
        if masked:
            w = jnp.where(before, w, 0.0)
        acc = acc + jnp.concatenate([_dot(w[h * nq:(h + 1) * nq], v_heads[h]) for h in range(n_heads)], axis=1)
        r_run = r_run + jnp.sum(log_keep, axis=1, keepdims=True)
        return r_run, acc

    kn, vn = kn_ref[...], vn_ref[...]
    r_run, acc = process([_pad_rows(kn[:, h * LANES:(h + 1) * LANES], page) for h in range(n_heads)],
                         [_pad_rows(vn[:, h * LANES:(h + 1) * LANES], page) for h in range(n_heads)],
                         jnp.zeros((rows, 1), F32), jnp.zeros((nq, n_heads * LANES), F32), True)

    def copies(p):
        pg = pt_ref[b, p]
        return (pltpu.make_async_copy(kc_ref.at[layer, pg], kbuf_ref, sem_ref.at[0]),
                pltpu.make_async_copy(vc_ref.at[layer, pg], vbuf_ref, sem_ref.at[1]))

    def cond(c):
        p, r_c, _ = c
        return jnp.logical_and(p >= 0, jnp.max(r_c) >= SB_EXIT_LOG)

    def body(c):
        p, r_c, a_c = c
        ck, cv = copies(p)
        ck.start()
        cv.start()
        ck.wait()
        cv.wait()
        r_c, a_c = process([kbuf_ref[pl.ds(h, page, stride=n_heads), :] for h in range(n_heads)],
                           [vbuf_ref[pl.ds(h, page, stride=n_heads), :] for h in range(n_heads)],
                           r_c, a_c, False)
        return p - 1, r_c, a_c

    _, _, acc = lax.while_loop(cond, body, (n_pages - 1, r_run, acc))
    o_ref[...] = acc.astype(o_ref.dtype)


def _sb_sample(page_table, q, kn3, vn3, kcache, vcache, layer, n_heads):
    n_req, n_pages = page_table.shape
    m, width = q.shape
    nq = m // n_req
    page = kcache.shape[2] // n_heads
    new_spec = pl.BlockSpec((None, nq, width), lambda b, pt: (layer, b, 0))
    any_spec = pl.BlockSpec(memory_space=pl.ANY)
    grid_spec = pltpu.PrefetchScalarGridSpec(
        num_scalar_prefetch=1,
        grid=(n_req,),
        in_specs=[pl.BlockSpec((nq, width), lambda b, pt: (b, 0)), new_spec, new_spec, any_spec, any_spec],
        out_specs=pl.BlockSpec((nq, width), lambda b, pt: (b, 0)),
        scratch_shapes=[pltpu.VMEM((page * n_heads, LANES), F32), pltpu.VMEM((page * n_heads, LANES), F32),
                        pltpu.SemaphoreType.DMA((2,))])
    return pl.pallas_call(
        functools.partial(_sb_sample_kernel, layer=layer, n_pages=n_pages, page=page, n_heads=n_heads,
                          scale=LANES ** -0.5),
        grid_spec=grid_spec,
        out_shape=jax.ShapeDtypeStruct((m, width), F32),
        compiler_params=_cparams(("arbitrary",)),
        name="sb_sample",
    )(page_table, q, kn3, vn3, kcache, vcache)


def _cum_matrices(page, n_heads):
    idx = np.arange(page * n_heads)
    key, head = idx // n_heads, idx % n_heads
    same = head[:, None] == head[None, :]
    w_cum = same & (key[:, None] <= key[None, :])
    return jnp.asarray(w_cum, BF16), jnp.asarray(same, BF16)


def _fox_cum_kernel(pt_ref, lfn_ref, wc_ref, wt_ref, lc_ref, cum_ref, cumn_ref, tot_ref, xbuf_ref, sem_ref,
                    *, layer, n_pages, rps):
    step = pl.program_id(0)
    rows = rps * n_pages

    def start(i, c):
        page = pt_ref[step * rps + lax.div(i, n_pages), lax.rem(i, n_pages)]
        pltpu.make_async_copy(lc_ref.at[layer, page], xbuf_ref.at[pl.ds(i, 1)], sem_ref.at[0]).start()
        return c

    def wait(i, c):
        pltpu.make_async_copy(lc_ref.at[layer, 0], xbuf_ref.at[pl.ds(i, 1)], sem_ref.at[0]).wait()
        return c

    lax.fori_loop(0, rows, start, 0)
    lax.fori_loop(0, rows, wait, 0)

    x = xbuf_ref[...]
    within = _dot_exact_r(x, wc_ref[...], 3)
    totals = _dot_exact_r(x, wt_ref[...], 3)
    pr = lax.broadcasted_iota(jnp.int32, (rows, rows), 0)
    pc = lax.broadcasted_iota(jnp.int32, (rows, rows), 1)
    earlier = jnp.logical_and(pc < pr, pr // n_pages == pc // n_pages)
    carry = _dot_exact_l(jnp.where(earlier, 1.0, 0.0).astype(BF16), totals, 3)
    cum_ref[...] = within + carry
    sr = lax.broadcasted_iota(jnp.int32, (2 * SUBLANES, rows), 0)
    sc = lax.broadcasted_iota(jnp.int32, (2 * SUBLANES, rows), 1)
    own = jnp.where(sc // n_pages == sr, 1.0, 0.0).astype(BF16)
    tot = _dot_exact_l(own, totals, 3)[0:rps]
    tot_ref[...] = tot
    lfn = lfn_ref[...]
    lfn = jnp.concatenate([lfn, jnp.zeros((2 * SUBLANES - rps, lfn.shape[1]), F32)], axis=0)
    cumn_ref[...] = tot + _dot_exact_r(lfn, wc_ref[...], 3)[0:rps]


def _fox_cum(page_table, lf_new, lcache, layer, page, n_heads):
    n_req, n_pages = page_table.shape
    w = page * n_heads
    rps = SUBLANES if n_req % SUBLANES == 0 else n_req
    rows = rps * n_pages
    wc, wt = _cum_matrices(page, n_heads)
    const = lambda shape: pl.BlockSpec(shape, lambda b, pt: (0,) * len(shape))
    grid_spec = pltpu.PrefetchScalarGridSpec(
        num_scalar_prefetch=1,
        grid=(n_req // rps,),
        in_specs=[pl.BlockSpec((rps, w), lambda b, pt: (b, 0)),
                  const((w, w)), const((w, w)),
                  pl.BlockSpec(memory_space=pl.ANY)],
        out_specs=[pl.BlockSpec((rows, w), lambda b, pt: (b, 0)),
                   pl.BlockSpec((rps, w), lambda b, pt: (b, 0)),
                   pl.BlockSpec((rps, w), lambda b, pt: (b, 0))],
        scratch_shapes=[pltpu.VMEM((rows, w), F32), pltpu.SemaphoreType.DMA((1,))])
    cum, cum_new, tot = pl.pallas_call(
        functools.partial(_fox_cum_kernel, layer=layer, n_pages=n_pages, rps=rps),
        grid_spec=grid_spec,
        out_shape=[jax.ShapeDtypeStruct((n_req * n_pages, w), F32),
                   jax.ShapeDtypeStruct((n_req, w), F32),
                   jax.ShapeDtypeStruct((n_req, w), F32)],
        compiler_params=_cparams(("arbitrary",)),
        name="fox_cum",
    )(page_table, lf_new, wc, wt, lcache)
    return cum.reshape(n_req, n_pages, w), cum_new.reshape(n_req, 1, w), tot.reshape(n_req, 1, w)


def _fox_sample_kernel(pt_ref, q_ref, kn_ref, vn_ref, *refs, n_pages, grp, n_heads, scale):
    k_refs, v_refs = refs[:grp], refs[grp:2 * grp]
    cum_ref, cumn_ref, tot_ref, o_ref, m_ref, l_ref, acc_ref = refs[2 * grp:]
    p = pl.program_id(1)
    n_steps = n_pages // grp
    nq = q_ref.shape[0]
    rows = n_heads * nq
    w = k_refs[0].shape[0]
    q = q_ref[...] * scale
    qall = jnp.concatenate([q[:, h * LANES:(h + 1) * LANES] for h in range(n_heads)], axis=0)
    lane = lax.broadcasted_iota(jnp.int32, (rows, w), 1)
    rowi = lax.broadcasted_iota(jnp.int32, (rows, w), 0)
    own_head = (lane % n_heads) == (rowi // nq)
    tot = tot_ref[...]

    @pl.when(p == 0)
    def _():
        m_ref[...] = jnp.full(m_ref.shape, NEG_INF, F32)
        l_ref[...] = jnp.zeros(l_ref.shape, F32)
        acc_ref[...] = jnp.zeros(acc_ref.shape, F32)

    def update(scores, values):
        m_old = m_ref[...]
        m_new = m_old
        for s in scores:
            m_new = jnp.maximum(m_new, jnp.max(s, axis=1, keepdims=True))
        alpha = jnp.exp(m_old - m_new)
        l_new = alpha * l_ref[...]
        pv = None
        for s, v in zip(scores, values):
            pr = jnp.exp(s - m_new)
            l_new = l_new + jnp.sum(pr, axis=1, keepdims=True)
            d = _dot(pr, v)
            pv = d if pv is None else pv + d
        m_ref[...] = m_new
        l_ref[...] = l_new
        acc_ref[...] = alpha * acc_ref[...] + pv

    @pl.when(p < n_steps)
    def _():
        scores = []
        for g in range(grp):
            ck = cum_ref[pl.ds(p * grp + g, 1), :]
            s = _dot_nt(qall, k_refs[g][...]) + (tot - ck)
            scores.append(jnp.where(own_head, s, NEG_INF))
        update(scores, [v_refs[g][...] for g in range(grp)])

    @pl.when(p == n_steps)
    def _():
        s = _dot_nt(qall, _pad_rows(kn_ref[...], w)) + (tot - cumn_ref[...])
        causal = jnp.logical_and(own_head, (lane // n_heads) <= (rowi % nq))
        update([jnp.where(causal, s, NEG_INF)], [_pad_rows(vn_ref[...], w)])
        out = acc_ref[...] / l_ref[...]
        for h in range(n_heads):
            o_ref[:, h * LANES:(h + 1) * LANES] = out[h * nq:(h + 1) * nq].astype(o_ref.dtype)


def _fox_sample(page_table, q, kn2, vn2, kcache, vcache, cum_past, cum_new, tot, layer, n_heads):
    n_req, n_pages = page_table.shape
    m, width = q.shape
    nq = m // n_req
    w = kcache.shape[2]
    grp = 8 if n_pages % 8 == 0 else (4 if n_pages % 4 == 0 else 1)
    n_steps = n_pages // grp

    def page_spec(g):
        return pl.BlockSpec((None, None, w, LANES),
                            lambda b, p, pt: (layer, pt[b, jnp.minimum(p, n_steps - 1) * grp + g], 0, 0))

    new_spec = pl.BlockSpec((nq * n_heads, LANES), lambda b, p, pt: (b, 0))
    grid_spec = pltpu.PrefetchScalarGridSpec(
        num_scalar_prefetch=1,
        grid=(n_req, n_steps + 1),
        in_specs=[pl.BlockSpec((nq, width), lambda b, p, pt: (b, 0)), new_spec, new_spec]
                 + [page_spec(g) for g in range(grp)] + [page_spec(g) for g in range(grp)]
                 + [pl.BlockSpec((None, n_pages, w), lambda b, p, pt: (b, 0, 0)),
                    pl.BlockSpec((None, 1, w), lambda b, p, pt: (b, 0, 0)),
                    pl.BlockSpec((None, 1, w), lambda b, p, pt: (b, 0, 0))],
        out_specs=pl.BlockSpec((nq, width), lambda b, p, pt: (b, 0)),
        scratch_shapes=[pltpu.VMEM((n_heads * nq, 1), F32), pltpu.VMEM((n_heads * nq, 1), F32),
                        pltpu.VMEM((n_heads * nq, LANES), F32)])
    return pl.pallas_call(
        functools.partial(_fox_sample_kernel, n_pages=n_pages, grp=grp, n_heads=n_heads, scale=LANES ** -0.5),
        grid_spec=grid_spec,
        out_shape=jax.ShapeDtypeStruct((m, width), F32),
        compiler_params=_cparams(("arbitrary", "arbitrary")),
        name="fox_sample",
    )(page_table, q, kn2, vn2, *([kcache] * grp), *([vcache] * grp), cum_past, cum_new, tot)


def _block_diag_gates(wa, wx, grp):
    g, w, _ = wa.shape
    eye = jnp.eye(grp, dtype=wa.dtype)

    def bd(x):
        x = x.reshape(g // grp, grp, w, w)
        return jnp.einsum("cgij,gh->cgihj", x, eye).reshape(g // grp, grp * w, grp * w)

    return jnp.concatenate([bd(wa), bd(wx)], axis=2)


def _layer(xs, l, wts, dims, lru_state, sample, prev_kv):
    (norm1_g, w_in, w_gate, b_gate, wf_t, fox_bf, conv_w, conv_b, wg, lru_ba, lru_bx, lru_lambda,
     w_br_lru, w_br_sb, w_br_fox, w_o, norm2_g, w_up, w_down) = wts
    depth = w_in.shape[0]
    d = xs[0].shape[1]
    r = lru_lambda.shape[1]
    width = w_br_sb.shape[1]
    h = width // LANES
    off_sb = 2 * r
    off_fox = off_sb + 3 * width
    tms = [min(x.shape[0], 1024) for x in xs]

    xn = [_rmsnorm(x, norm1_g[l], BF16) for x in xs]
    grp = lambda acts, extras=None: [([a[g] for a in acts], [e[g] for e in (extras or [])], tms[g])
                                     for g in range(len(xs))]
    proj = lambda off, n, nm, **kw: _gmm(grp([xn]), [w_in], l, n, tn=1024, epilogue=_epi_plain, out_dtype=F32,
                                         col_off=off, name=nm, **kw)
    u_lru = proj(0, 2 * r, "proj_lru")
    sb_q = proj(off_sb, width, "proj_sbq")
    fox_q = proj(off_fox, width, "proj_fq")
    kv = []
    for idx, (off, nm) in enumerate([(off_sb + width, "proj_sbk"), (off_sb + 2 * width, "proj_sbv"),
                                     (off_fox + width, "proj_fk"), (off_fox + 2 * width, "proj_fv")]):
        tms_kv = [min(t, 512) for t in tms] if prev_kv is None else tms
        kv.append(_gmm([([xn[g]], [], tms_kv[g]) for g in range(len(xs))], [w_in], l, width, tn=1024,
                       epilogue=_epi_plain, out_dtype=F32, col_off=off, name=nm,
                       stack=(depth, None if prev_kv is None else prev_kv[idx])))
    gates = _gmm(grp([xn]), [w_gate], l, 3 * d, tn=1024, epilogue=_epi_sigmoid, out_dtype=F32,
                 bias=b_gate[l].reshape(1, 3 * d), name="proj_gate")

    branches, states = [], []
    for g, (n_seq, seq_len) in enumerate(dims):
        lf_t, cum_t = _forget(xn[g], wf_t[l], fox_bf[l], seq_len)
        h0, conv0 = lru_state[g]
        y_lru, h_last, conv_new = _lru(u_lru[g], h0, conv0, conv_w[l], conv_b[l], wg[l], lru_ba[l], lru_bx[l],
                                       lru_lambda[l], n_seq, seq_len)
        lf = lf_t.T
        if g == 0:
            o_sb = _prompt_attention(_sb_prompt_kernel, sb_q[g], kv[0][g], kv[1][g], l, n_seq, seq_len,
                                     name="sb_prompt")
            cum = cum_t.reshape(h, n_seq, 1, seq_len).transpose(1, 0, 2, 3)
            o_fox = _prompt_attention(_fox_prompt_kernel, fox_q[g], kv[2][g], kv[3][g], l, n_seq, seq_len,
                                      cum=cum, bk_max=512, name="fox_prompt")
        else:
            page_table, c_sb_k, c_sb_v, c_fox_k, c_fox_v, c_logf, page = sample
            o_sb = _sb_sample(page_table, sb_q[g], kv[0][g], kv[1][g], c_sb_k, c_sb_v, l, h)
            lf_new = jnp.pad(lf.reshape(n_seq, seq_len * h), ((0, 0), (0, (page - seq_len) * h)))
            cum_past, cum_new, tot = _fox_cum(page_table, lf_new, c_logf, l, page, h)
            rows = lambda a: a[l].reshape(n_seq * seq_len * h, LANES)
            o_fox = _fox_sample(page_table, fox_q[g], rows(kv[2][g]), rows(kv[3][g]), c_fox_k, c_fox_v,
                                cum_past, cum_new, tot, l, h)
        branches.append((y_lru.astype(BF16), o_sb.astype(BF16), o_fox.astype(BF16)))
        states.append((lf.reshape(n_seq, seq_len, h), h_last, conv_new))

    acts = [[b[i] for b in branches] for i in range(3)]
    nb = d // 512
    tms_mid = [min(t, 512) for t in tms]
    merged = _gmm([([a[g] for a in acts], [gates[g]] * 3, tms_mid[g]) for g in range(len(xs))],
                  [w_br_lru, w_br_sb, w_br_fox], l, d, tn=512, epilogue=_epi_gated_sum,
                  out_dtype=BF16, extra_offs=(0, nb, 2 * nb), name="merge")
    x1 = _gmm([([merged[g]], [xs[g]], tms_mid[g]) for g in range(len(xs))], [w_o], l, d, tn=1024,
              epilogue=_epi_residual, out_dtype=F32, extra_offs=(0,), name="out_proj")
    hn = [_rmsnorm(x, norm2_g[l], BF16) for x in x1]
    hid = _gmm(grp([hn]), [w_up], l, w_up.shape[2], tn=1024, epilogue=_epi_relu2, out_dtype=BF16, name="mlp_up")
    tms_down = [min(t, 256) for t in tms]
    x2 = _gmm([([hid[g]], [x1[g]], tms_down[g]) for g in range(len(xs))], [w_down], l, d, tn=512,
              epilogue=_epi_residual, out_dtype=F32, extra_offs=(0,), single_buffer_w=True, name="mlp_down")
    return x2, kv, states


def kernel(x_prompt, x_sample, cache_sb_k, cache_sb_v, cache_fox_k, cache_fox_v, cache_fox_logf, state_lru_h, state_conv, page_table, norm1_g, w_in, b_gate, conv_w, conv_b, lru_wa, lru_ba, lru_wx, lru_bx, lru_lambda, fox_bf, w_br_lru, w_br_sb, w_br_fox, w_o, norm2_g, w_up, w_down, final_g):
    depth = w_in.shape[0]
    n_p, t_p, d = x_prompt.shape
    n_s, t_s, _ = x_sample.shape
    r = lru_lambda.shape[1]
    n_pool, page, h_sb, hd = cache_sb_k.shape[1:]
    h_fox = cache_fox_k.shape[3]
    assert hd == LANES and h_sb == h_fox == SUBLANES
    width = h_sb * hd
    off_f = 2 * r + 6 * width
    off_gate = off_f + h_fox

    w_gate = w_in[:, :, off_gate:]
    wf_t = jnp.pad(jnp.swapaxes(w_in[:, :, off_f:off_gate], 1, 2), ((0, 0), (0, 2 * SUBLANES - h_fox), (0, 0)))
    wg = jax.vmap(lambda a, b: _block_diag_gates(a, b, 4))(lru_wa, lru_wx)
    wts = (norm1_g, w_in, w_gate, b_gate, wf_t, fox_bf, conv_w, conv_b, wg, lru_ba, lru_bx, lru_lambda,
           w_br_lru, w_br_sb, w_br_fox, w_o, norm2_g, w_up, w_down)

    rows = lambda c: c.reshape(depth, n_pool, page * h_sb, hd)
    sample = (page_table, rows(cache_sb_k), rows(cache_sb_v), rows(cache_fox_k), rows(cache_fox_v),
              cache_fox_logf.reshape(depth, n_pool, 1, page * h_fox), page)

    xs = [x_prompt.reshape(n_p * t_p, d), x_sample.reshape(n_s * t_s, d)]
    dims = [(n_p, t_p), (n_s, t_s)]
    kv = None
    small = []
    for l in range(depth):
        lru_state = [(jnp.zeros((n_p, r), F32), jnp.zeros((n_p, conv_w.shape[1] - 1, r), F32)),
                     (state_lru_h[l], state_conv[l])]
        xs, kv, states = _layer(xs, l, wts, dims, lru_state, sample, kv)
        small.append(states)

    y = [_rmsnorm(x, final_g, F32).reshape(n, t, d) for x, (n, t) in zip(xs, dims)]
    outs = []
    for g, (n, t) in enumerate(dims):
        outs.extend(a[g].reshape(depth, n, t, h_sb, hd) for a in kv)
        outs.extend(jnp.stack([small[l][g][i] for l in range(depth)]) for i in range(3))
    return (y[0], y[1], *outs)
```

```python
import functools

import numpy as np
import jax
import jax.numpy as jnp
from jax import lax
from jax.experimental import pallas as pl
from jax.experimental.pallas import tpu as pltpu

F32 = jnp.float32
BF16 = jnp.bfloat16

RMS_EPS = 1e-6
LRU_C = 8.0
NEG_INF = -1e30
LANES = 128
SUBLANES = 8
SB_EXIT_LOG = -104.0
VMEM_LIMIT = 56 * 1024 * 1024


def _cparams(sem, vmem=VMEM_LIMIT):
    return pltpu.CompilerParams(dimension_semantics=sem, vmem_limit_bytes=vmem)


def _log_sigmoid(z):
    return jnp.minimum(z, 0.0) - jnp.log1p(jnp.exp(-jnp.abs(z)))


def _dot(a, b):
    return jnp.dot(a, b, preferred_element_type=F32)


def _dot_nt(a, b):
    return lax.dot_general(a, b, (((1,), (1,)), ((), ())), preferred_element_type=F32)


def _dot_exact_r(x, m01, parts):
    out = None
    r = x
    for p in range(parts):
        h = r.astype(BF16)
        d = _dot(h, m01)
        out = d if out is None else out + d
        if p + 1 < parts:
            r = r - h.astype(F32)
    return out


def _dot_exact_l(m01, x, parts):
    out = None
    r = x
    for p in range(parts):
        h = r.astype(BF16)
        d = _dot(m01, h)
        out = d if out is None else out + d
        if p + 1 < parts:
            r = r - h.astype(F32)
    return out


def _tri(n, kind):
    r = lax.broadcasted_iota(jnp.int32, (n, n), 0)
    c = lax.broadcasted_iota(jnp.int32, (n, n), 1)
    if kind == "row_gt_col":
        m = r > c
    elif kind == "row_le_col":
        m = r <= c
    else:
        raise ValueError(kind)
    return jnp.where(m, 1.0, 0.0).astype(BF16)


def _rmsnorm_kernel(x_ref, g_ref, o_ref):
    x = x_ref[...]
    ms = jnp.mean(x * x, axis=-1, keepdims=True)
    o_ref[...] = (x * lax.rsqrt(ms + RMS_EPS) * g_ref[...]).astype(o_ref.dtype)


def _rmsnorm(x, g, out_dtype):
    m, d = x.shape
    tm = min(m, 256)
    return pl.pallas_call(
        _rmsnorm_kernel,
        grid=(m // tm,),
        in_specs=[pl.BlockSpec((tm, d), lambda i: (i, 0)),
                  pl.BlockSpec((1, d), lambda i: (0, 0))],
        out_specs=pl.BlockSpec((tm, d), lambda i: (i, 0)),
        out_shape=jax.ShapeDtypeStruct((m, d), out_dtype),
        compiler_params=_cparams(("parallel",)),
        name="rmsnorm",
    )(x, g.reshape(1, d))


def _epi_plain(dots, extras, bias):
    return dots[0]


def _epi_sigmoid(dots, extras, bias):
    return jax.nn.sigmoid(dots[0] + bias)


def _epi_relu2(dots, extras, bias):
    return jnp.square(jnp.maximum(dots[0], 0.0))


def _epi_residual(dots, extras, bias):
    return extras[0] + dots[0]


def _epi_gated_sum(dots, extras, bias):
    out = extras[0] * dots[0]
    for e, d in zip(extras[1:], dots[1:]):
        out = out + e * d
    return out


def _gmm_kernel(*refs, n_act, n_extra, n_alias, steps, epilogue, fill_layer):
    n_grp = len(steps)
    pos = 0
    acts = [refs[pos + g * n_act: pos + (g + 1) * n_act] for g in range(n_grp)]
    pos += n_grp * n_act
    extras = [refs[pos + g * n_extra: pos + (g + 1) * n_extra] for g in range(n_grp)]
    pos += n_grp * n_extra
    w_refs = refs[pos:pos + n_act]
    pos += n_act
    b_ref = refs[pos]
    pos += 1 + n_alias
    o_refs = refs[pos:pos + n_grp]
    wb_refs = refs[pos + n_grp:]

    i = pl.program_id(1)

    @pl.when(i == 0)
    def _():
        for w, wb in zip(w_refs, wb_refs):
            wb[...] = w[...].astype(BF16)

    for g, (start, count) in enumerate(steps):
        @pl.when(jnp.logical_and(i >= start, i < start + count))
        def _(g=g):
            dots = [_dot(a[...], wb[...]) for a, wb in zip(acts[g], wb_refs)]
            out = epilogue(dots, [e[...] for e in extras[g]], b_ref[...]).astype(o_refs[g].dtype)
            if fill_layer is None:
                o_refs[g][...] = out
            else:
                for k in range(o_refs[g].shape[0]):
                    o_refs[g][k] = out if k == fill_layer else jnp.zeros_like(out)


def _gmm(groups, weights, layer, n, *, tn, epilogue, out_dtype, col_off=0, bias=None, extra_offs=(),
         single_buffer_w=False, stack=None, name="gmm"):
    assert col_off % tn == 0 and n % tn == 0
    off = col_off // tn
    n_act = len(weights)
    n_extra = len(extra_offs)
    steps, start = [], 0
    for acts, extras, tm in groups:
        m = acts[0].shape[0]
        assert m % tm == 0 and len(acts) == n_act and len(extras) == n_extra
        steps.append((start, m // tm))
        start += m // tm

    def row(g):
        s, c = steps[g]
        return lambda i: jnp.clip(i - s, 0, c - 1)

    in_specs, args = [], []
    for g, (acts, extras, tm) in enumerate(groups):
        for a in acts:
            in_specs.append(pl.BlockSpec((tm, a.shape[1]), lambda j, i, r=row(g): (r(i), 0)))
            args.append(a)
    for g, (acts, extras, tm) in enumerate(groups):
        for e, eo in zip(extras, extra_offs):
            in_specs.append(pl.BlockSpec((tm, tn), lambda j, i, r=row(g), eo=eo: (r(i), j + eo)))
            args.append(e)
    w_kw = dict(pipeline_mode=pl.Buffered(1)) if single_buffer_w else {}
    for w in weights:
        in_specs.append(pl.BlockSpec((None, w.shape[1], tn), lambda j, i: (layer, 0, off + j), **w_kw))
        args.append(w)
    if bias is None:
        bias = jnp.zeros((1, n), F32)
    in_specs.append(pl.BlockSpec((1, tn), lambda j, i: (0, j)))
    args.append(bias)

    aliases = {}
    n_alias = 0
    fill_layer = None
    if stack is not None:
        depth, prev = stack
        out_shape = [jax.ShapeDtypeStruct((depth, acts[0].shape[0], n), out_dtype) for acts, _, _ in groups]
        if prev is None:
            fill_layer = layer
            out_specs = [pl.BlockSpec((depth, tm, tn), lambda j, i, r=row(g): (0, r(i), j))
                         for g, (_, _, tm) in enumerate(groups)]
        else:
            out_specs = [pl.BlockSpec((None, tm, tn), lambda j, i, r=row(g): (layer, r(i), j))
                         for g, (_, _, tm) in enumerate(groups)]
            for g, p in enumerate(prev):
                aliases[len(args)] = g
                in_specs.append(pl.BlockSpec(memory_space=pl.ANY))
                args.append(p)
                n_alias += 1
    else:
        out_specs = [pl.BlockSpec((tm, tn), lambda j, i, r=row(g): (r(i), j))
                     for g, (_, _, tm) in enumerate(groups)]
        out_shape = [jax.ShapeDtypeStruct((acts[0].shape[0], n), out_dtype) for acts, _, _ in groups]

    return pl.pallas_call(
        functools.partial(_gmm_kernel, n_act=n_act, n_extra=n_extra, n_alias=n_alias, steps=tuple(steps),
                          epilogue=epilogue, fill_layer=fill_layer),
        grid=(n // tn, start),
        in_specs=in_specs,
        out_specs=out_specs,
        out_shape=out_shape,
        scratch_shapes=[pltpu.VMEM((w.shape[1], tn), BF16) for w in weights],
        input_output_aliases=aliases,
        compiler_params=_cparams(("arbitrary", "arbitrary")),
        name=name,
    )(*args)


def _forget_kernel(x_ref, wf_ref, bf_ref, lf_ref, cum_ref, carry_ref, *, tiles_per_seq, n_heads):
    i = pl.program_id(0)

    @pl.when(i % tiles_per_seq == 0)
    def _():
        carry_ref[...] = jnp.zeros_like(carry_ref)

    f = _dot_nt(wf_ref[...].astype(BF16), x_ref[...])[0:n_heads]
    lf = _log_sigmoid(f + bf_ref[...])
    lf_ref[...] = lf
    tm = lf.shape[1]
    cs = _dot_exact_r(lf, _tri(tm, "row_le_col"), 3) + carry_ref[...]
    cum_ref[...] = cs
    carry_ref[...] = cs[:, tm - 1:tm]


def _forget(xn, wf_t, bf, seq_len):
    m, k = xn.shape
    h = bf.shape[0]
    tm = min(seq_len, 256) if seq_len % 256 == 0 else m
    hp = wf_t.shape[0]
    return pl.pallas_call(
        functools.partial(_forget_kernel, tiles_per_seq=max(seq_len // tm, 1), n_heads=h),
        grid=(m // tm,),
        in_specs=[pl.BlockSpec((tm, k), lambda i: (i, 0)),
                  pl.BlockSpec((hp, k), lambda i: (0, 0)),
                  pl.BlockSpec((h, 1), lambda i: (0, 0))],
        out_specs=[pl.BlockSpec((h, tm), lambda i: (0, i)),
                   pl.BlockSpec((h, tm), lambda i: (0, i))],
        out_shape=[jax.ShapeDtypeStruct((h, m), F32), jax.ShapeDtypeStruct((h, m), F32)],
        scratch_shapes=[pltpu.VMEM((h, 1), F32)],
        compiler_params=_cparams(("arbitrary",)),
        name="forget",
    )(xn, wf_t, bf.reshape(h, 1))


def _lru_kernel(u_ref, h0_ref, c0_ref, cw_ref, cb_ref, wg_ref, ba_ref, bx_ref, lam_ref,
                y_ref, hl_ref, cn_ref, xbuf_ref, hc_ref, *, tt, r, n_grp, conv_w):
    j = pl.program_id(1)
    pad = SUBLANES

    @pl.when(j == 0)
    def _():
        xbuf_ref[0:pad, :] = jnp.zeros((pad, r), F32)
        xbuf_ref[pad - (conv_w - 1):pad, :] = c0_ref[...]
        hc_ref[...] = h0_ref[...]

    x = u_ref[:, 0:r]
    gate_in = u_ref[:, r:2 * r]
    xbuf_ref[pad:pad + tt, :] = x
    cw = cw_ref[...]
    xc = cb_ref[...] + x * cw[conv_w - 1:conv_w, :]
    for tap in range(conv_w - 1):
        shift = conv_w - 1 - tap
        xc = xc + xbuf_ref[pad - shift:pad - shift + tt, :] * cw[tap:tap + 1, :]

    gw = r // n_grp
    pre = []
    for c in range(n_grp):
        pre.append(_dot(xc[:, c * gw:(c + 1) * gw].astype(BF16), wg_ref[c].astype(BF16)))
    pa = jnp.concatenate([p[:, 0:gw] for p in pre], axis=1)
    px = jnp.concatenate([p[:, gw:2 * gw] for p in pre], axis=1)
    rg = jax.nn.sigmoid(pa + ba_ref[...])
    ig = jax.nn.sigmoid(px + bx_ref[...])
    log_a = LRU_C * rg * _log_sigmoid(lam_ref[...])
    a = jnp.exp(log_a)
    th = jnp.tanh(log_a)
    mult = jnp.sqrt(-2.0 * th / (1.0 - th))
    b = mult * (ig * xc)

    in_grp = lax.broadcasted_iota(jnp.int32, (tt, r), 0) % SUBLANES
    s = 1
    while s < SUBLANES:
        a_s = pltpu.roll(a, s, 0)
        b_s = pltpu.roll(b, s, 0)
        keep = in_grp >= s
        b = jnp.where(keep, a * b_s + b, b)
        a = jnp.where(keep, a * a_s, a)
        s *= 2
    h_prev = hc_ref[...]
    h_grps = []
    for g in range(tt // SUBLANES):
        rows = slice(g * SUBLANES, (g + 1) * SUBLANES)
        h_g = b[rows] + a[rows] * h_prev
        h_grps.append(h_g)
        h_prev = h_g[SUBLANES - 1:SUBLANES]
    h = jnp.concatenate(h_grps, axis=0)
    y_ref[...] = (h * jax.nn.gelu(gate_in)).astype(y_ref.dtype)
    hc_ref[...] = h[tt - 1:tt, :]
    xbuf_ref[0:pad, :] = xbuf_ref[tt:tt + pad, :]

    @pl.when(j == pl.num_programs(1) - 1)
    def _():
        hl_ref[...] = h[tt - 1:tt, :]
        cn_ref[...] = xbuf_ref[pad - (conv_w - 1):pad, :]


def _lru(u_lru, h0, conv0, conv_w, conv_b, wg, ba, bx, lam, n_seq, seq_len):
    m, r2 = u_lru.shape
    r = r2 // 2
    cwid = conv_w.shape[0]
    tt = min(seq_len, 256)
    nt = seq_len // tt
    n_grp = wg.shape[0]
    row = lambda a: a.reshape(1, r)
    y, hl, cn = pl.pallas_call(
        functools.partial(_lru_kernel, tt=tt, r=r, n_grp=n_grp, conv_w=cwid),
        grid=(n_seq, nt),
        in_specs=[pl.BlockSpec((tt, r2), lambda b, j: (b * nt + j, 0)),
                  pl.BlockSpec((None, 1, r), lambda b, j: (b, 0, 0)),
                  pl.BlockSpec((None, cwid - 1, r), lambda b, j: (b, 0, 0)),
                  pl.BlockSpec((cwid, r), lambda b, j: (0, 0)),
                  pl.BlockSpec((1, r), lambda b, j: (0, 0)),
                  pl.BlockSpec(wg.shape, lambda b, j: (0, 0, 0)),
                  pl.BlockSpec((1, r), lambda b, j: (0, 0)),
                  pl.BlockSpec((1, r), lambda b, j: (0, 0)),
                  pl.BlockSpec((1, r), lambda b, j: (0, 0))],
        out_specs=[pl.BlockSpec((tt, r), lambda b, j: (b * nt + j, 0)),
                   pl.BlockSpec((None, 1, r), lambda b, j: (b, 0, 0)),
                   pl.BlockSpec((None, cwid - 1, r), lambda b, j: (b, 0, 0))],
        out_shape=[jax.ShapeDtypeStruct((m, r), BF16 if tt % (2 * SUBLANES) == 0 else F32),
                   jax.ShapeDtypeStruct((n_seq, 1, r), F32),
                   jax.ShapeDtypeStruct((n_seq, cwid - 1, r), F32)],
        scratch_shapes=[pltpu.VMEM((tt + 2 * SUBLANES, r), F32), pltpu.VMEM((1, r), F32)],
        compiler_params=_cparams(("arbitrary", "arbitrary")),
        name="lru",
    )(u_lru, h0.reshape(n_seq, 1, r), conv0, conv_w, row(conv_b), wg, row(ba), row(bx), row(lam))
    return y, hl.reshape(n_seq, r), cn


def _cast_kv(k_ref, v_ref, kb_ref, vb_ref):
    kb_ref[...] = k_ref[...].astype(BF16)
    vb_ref[...] = v_ref[...].astype(BF16)


def _sb_prompt_kernel(q_ref, k_ref, v_ref, o_ref, kb_ref, vb_ref, *, bq, bk, hp, scale):
    qi = pl.program_id(2)
    ratio = bq // bk

    @pl.when(qi == 0)
    def _():
        _cast_kv(k_ref, v_ref, kb_ref, vb_ref)

    q_all = q_ref[...] * scale
    qs = [q_all[:, h * LANES:(h + 1) * LANES].astype(BF16) for h in range(hp)]
    row = lax.broadcasted_iota(jnp.int32, (bq, bk), 0)
    col = lax.broadcasted_iota(jnp.int32, (bq, bk), 1)
    upper = _tri(bk, "row_gt_col")

    def block(kb, carry, sub):
        start = pl.multiple_of(kb * bk, bk)
        out = []
        for h, (r_run, acc) in enumerate(carry):
            ks = kb_ref[pl.ds(start, bk), h * LANES:(h + 1) * LANES]
            vs = vb_ref[pl.ds(start, bk), h * LANES:(h + 1) * LANES]
            z = _dot_nt(qs[h], ks)
            log_beta = _log_sigmoid(z)
            log_keep = log_beta - z
            if sub is not None:
                before = col + sub * bk < row
                log_keep = jnp.where(before, log_keep, 0.0)
            later = _dot_exact_r(log_keep, upper, 2) + r_run
            w = jnp.exp(log_beta + later)
            if sub is not None:
                w = jnp.where(before, w, 0.0)
            acc = acc + _dot(w.astype(BF16), vs)
            r_run = r_run + jnp.sum(log_keep, axis=1, keepdims=True)
            out.append((r_run, acc))
        return tuple(out)

    carry = tuple((jnp.zeros((bq, 1), F32), jnp.zeros((bq, LANES), F32)) for _ in range(hp))
    for sub in reversed(range(ratio)):
        carry = block(qi * ratio + sub, carry, sub)

    def cond(c):
        kb, heads = c
        live = jnp.max(heads[0][0])
        for r_c, _ in heads[1:]:
            live = jnp.maximum(live, jnp.max(r_c))
        return jnp.logical_and(kb >= 0, live >= SB_EXIT_LOG)

    def body(c):
        kb, heads = c
        return kb - 1, block(kb, heads, None)

    _, carry = lax.while_loop(cond, body, (qi * ratio - 1, carry))
    for h, (_, acc) in enumerate(carry):
        o_ref[:, h * LANES:(h + 1) * LANES] = acc.astype(o_ref.dtype)


def _fox_prompt_kernel(q_ref, k_ref, v_ref, c_ref, o_ref, kb_ref, vb_ref, *, bq, bk, hp, scale):
    qi = pl.program_id(2)
    ratio = bq // bk

    @pl.when(qi == 0)
    def _():
        _cast_kv(k_ref, v_ref, kb_ref, vb_ref)

    q_all = q_ref[...] * scale
    qs = [q_all[:, h * LANES:(h + 1) * LANES].astype(BF16) for h in range(hp)]
    row = lax.broadcasted_iota(jnp.int32, (bq, bk), 0)
    col = lax.broadcasted_iota(jnp.int32, (bq, bk), 1)
    c0 = [c_ref[h, :, pl.ds(pl.multiple_of(qi * bq, bq), bk)][:, 0:1] for h in range(hp)]

    def block(kb, carry, sub):
        start = pl.multiple_of(kb * bk, bk)
        out = []
        for h, (m_run, l_run, acc) in enumerate(carry):
            ks = kb_ref[pl.ds(start, bk), h * LANES:(h + 1) * LANES]
            vs = vb_ref[pl.ds(start, bk), h * LANES:(h + 1) * LANES]
            s = _dot_nt(qs[h], ks) + (c0[h] - c_ref[h, :, pl.ds(start, bk)])
            if sub is not None:
                s = jnp.where(col + sub * bk <= row, s, NEG_INF)
            m_new = jnp.maximum(m_run, jnp.max(s, axis=1, keepdims=True))
            alpha = jnp.exp(m_run - m_new)
            p = jnp.exp(s - m_new)
            l_run = alpha * l_run + jnp.sum(p, axis=1, keepdims=True)
            acc = alpha * acc + _dot(p.astype(BF16), vs)
            out.append((m_new, l_run, acc))
        return tuple(out)

    carry = tuple((jnp.full((bq, 1), NEG_INF, F32), jnp.zeros((bq, 1), F32), jnp.zeros((bq, LANES), F32))
                  for _ in range(hp))
    carry = lax.fori_loop(0, qi * ratio, lambda kb, c: block(kb, c, None), carry)
    for sub in range(ratio):
        carry = block(qi * ratio + sub, carry, sub)
    for h, (_, l_run, acc) in enumerate(carry):
        o_ref[:, h * LANES:(h + 1) * LANES] = (acc / l_run).astype(o_ref.dtype)


def _prompt_attention(kernel, q, k3, v3, layer, n_seq, seq_len, cum=None, bq_max=512, bk_max=256, name="attn"):
    m, width = q.shape
    hp = 2
    h = width // (hp * LANES)
    bq = min(seq_len, bq_max)
    bk = min(seq_len, bk_max)
    nq = seq_len // bq
    scale = LANES ** -0.5
    kv_spec = pl.BlockSpec((None, seq_len, hp * LANES), lambda b, hh, i: (layer, b, hh))
    in_specs = [pl.BlockSpec((bq, hp * LANES), lambda b, hh, i: (b * nq + i, hh)), kv_spec, kv_spec]
    args = [q, k3, v3]
    if cum is not None:
        in_specs.append(pl.BlockSpec((None, hp, 1, seq_len), lambda b, hh, i: (b, hh, 0, 0)))
        args.append(cum)
    return pl.pallas_call(
        functools.partial(kernel, bq=bq, bk=bk, hp=hp, scale=scale),
        grid=(n_seq, h, nq),
        in_specs=in_specs,
        out_specs=pl.BlockSpec((bq, hp * LANES), lambda b, hh, i: (b * nq + i, hh)),
        out_shape=jax.ShapeDtypeStruct((m, width), BF16),
        scratch_shapes=[pltpu.VMEM((seq_len, hp * LANES), BF16), pltpu.VMEM((seq_len, hp * LANES), BF16)],
        compiler_params=_cparams(("arbitrary", "arbitrary", "arbitrary")),
        name=name,
    )(*args)


def _pad_rows(x, rows):
    return jnp.concatenate([x, jnp.zeros((rows - x.shape[0], x.shape[1]), x.dtype)], axis=0)


def _sb_sample_kernel(pt_ref, q_ref, kn_ref, vn_ref, kc_ref, vc_ref, o_ref, kbuf_ref, vbuf_ref, sem_ref,
                      *, layer, n_pages, page, n_heads, scale):
    b = pl.program_id(0)
    nq = q_ref.shape[0]
    rows = n_heads * nq
    q = q_ref[...] * scale
    qh = [q[:, h * LANES:(h + 1) * LANES] for h in range(n_heads)]
    qpos = lax.broadcasted_iota(jnp.int32, (rows, page), 0) % nq
    kidx = lax.broadcasted_iota(jnp.int32, (rows, page), 1)
    upper = _tri(page, "row_gt_col")

    def process(k_heads, v_heads, r_run, acc, masked):
        z = jnp.concatenate([_dot_nt(qh[h], k_heads[h]) for h in range(n_heads)], axis=0)
        log_beta = _log_sigmoid(z)
        log_keep = log_beta - z
        if masked:
            before = kidx < qpos
            log_keep = jnp.where(before, log_keep, 0.0)
        later = _dot_exact_r(log_keep, upper, 2) + r_run
        w = jnp.exp(log_beta + later)
        if masked:
            w = jnp.where(before, w, 0.0)
        acc = acc + jnp.concatenate([_dot(w[h * nq:(h + 1) * nq], v_heads[h]) for h in range(n_heads)], axis=1)
        r_run = r_run + jnp.sum(log_keep, axis=1, keepdims=True)
        return r_run, acc

    kn, vn = kn_ref[...], vn_ref[...]
    r_run, acc = process([_pad_rows(kn[:, h * LANES:(h + 1) * LANES], page) for h in range(n_heads)],
                         [_pad_rows(vn[:, h * LANES:(h + 1) * LANES], page) for h in range(n_heads)],
                         jnp.zeros((rows, 1), F32), jnp.zeros((nq, n_heads * LANES), F32), True)

    def copies(p):
        pg = pt_ref[b, p]
        return (pltpu.make_async_copy(kc_ref.at[layer, pg], kbuf_ref, sem_ref.at[0]),
                pltpu.make_async_copy(vc_ref.at[layer, pg], vbuf_ref, sem_ref.at[1]))

    def cond(c):
        p, r_c, _ = c
        return jnp.logical_and(p >= 0, jnp.max(r_c) >= SB_EXIT_LOG)

    def body(c):
        p, r_c, a_c = c
        ck, cv = copies(p)
        ck.start()
        cv.start()
        ck.wait()
        cv.wait()
        r_c, a_c = process([kbuf_ref[pl.ds(h, page, stride=n_heads), :] for h in range(n_heads)],
                           [vbuf_ref[pl.ds(h, page, stride=n_heads), :] for h in range(n_heads)],
                           r_c, a_c, False)
        return p - 1, r_c, a_c

    _, _, acc = lax.while_loop(cond, body, (n_pages - 1, r_run, acc))
    o_ref[...] = acc.astype(o_ref.dtype)


def _sb_sample(page_table, q, kn3, vn3, kcache, vcache, layer, n_heads):
    n_req, n_pages = page_table.shape
    m, width = q.shape
    nq = m // n_req
    page = kcache.shape[2] // n_heads
    new_spec = pl.BlockSpec((None, nq, width), lambda b, pt: (layer, b, 0))
    any_spec = pl.BlockSpec(memory_space=pl.ANY)
    grid_spec = pltpu.PrefetchScalarGridSpec(
        num_scalar_prefetch=1,
        grid=(n_req,),
        in_specs=[pl.BlockSpec((nq, width), lambda b, pt: (b, 0)), new_spec, new_spec, any_spec, any_spec],
        out_specs=pl.BlockSpec((nq, width), lambda b, pt: (b, 0)),
        scratch_shapes=[pltpu.VMEM((page * n_heads, LANES), F32), pltpu.VMEM((page * n_heads, LANES), F32),
                        pltpu.SemaphoreType.DMA((2,))])
    return pl.pallas_call(
        functools.partial(_sb_sample_kernel, layer=layer, n_pages=n_pages, page=page, n_heads=n_heads,
                          scale=LANES ** -0.5),
        grid_spec=grid_spec,
        out_shape=jax.ShapeDtypeStruct((m, width), F32),
        compiler_params=_cparams(("arbitrary",)),
        name="sb_sample",
    )(page_table, q, kn3, vn3, kcache, vcache)


def _cum_matrices(page, n_heads):
    idx = np.arange(page * n_heads)
    key, head = idx // n_heads, idx % n_heads
    same = head[:, None] == head[None, :]
    w_cum = same & (key[:, None] <= key[None, :])
    return jnp.asarray(w_cum, BF16), jnp.asarray(same, BF16)


def _fox_cum_kernel(pt_ref, lfn_ref, wc_ref, wt_ref, lc_ref, cum_ref, cumn_ref, tot_ref, xbuf_ref, sem_ref,
                    *, layer, n_pages, rps):
    step = pl.program_id(0)
    rows = rps * n_pages

    def start(i, c):
        page = pt_ref[step * rps + lax.div(i, n_pages), lax.rem(i, n_pages)]
        pltpu.make_async_copy(lc_ref.at[layer, page], xbuf_ref.at[pl.ds(i, 1)], sem_ref.at[0]).start()
        return c

    def wait(i, c):
        pltpu.make_async_copy(lc_ref.at[layer, 0], xbuf_ref.at[pl.ds(i, 1)], sem_ref.at[0]).wait()
        return c

    lax.fori_loop(0, rows, start, 0)
    lax.fori_loop(0, rows, wait, 0)

    x = xbuf_ref[...]
    within = _dot_exact_r(x, wc_ref[...], 3)
    totals = _dot_exact_r(x, wt_ref[...], 3)
    pr = lax.broadcasted_iota(jnp.int32, (rows, rows), 0)
    pc = lax.broadcasted_iota(jnp.int32, (rows, rows), 1)
    earlier = jnp.logical_and(pc < pr, pr // n_pages == pc // n_pages)
    carry = _dot_exact_l(jnp.where(earlier, 1.0, 0.0).astype(BF16), totals, 3)
    cum_ref[...] = within + carry
    sr = lax.broadcasted_iota(jnp.int32, (2 * SUBLANES, rows), 0)
    sc = lax.broadcasted_iota(jnp.int32, (2 * SUBLANES, rows), 1)
    own = jnp.where(sc // n_pages == sr, 1.0, 0.0).astype(BF16)
    tot = _dot_exact_l(own, totals, 3)[0:rps]
    tot_ref[...] = tot
    lfn = lfn_ref[...]
    lfn = jnp.concatenate([lfn, jnp.zeros((2 * SUBLANES - rps, lfn.shape[1]), F32)], axis=0)
    cumn_ref[...] = tot + _dot_exact_r(lfn, wc_ref[...], 3)[0:rps]


def _fox_cum(page_table, lf_new, lcache, layer, page, n_heads):
    n_req, n_pages = page_table.shape
    w = page * n_heads
    rps = SUBLANES if n_req % SUBLANES == 0 else n_req
    rows = rps * n_pages
    wc, wt = _cum_matrices(page, n_heads)
    const = lambda shape: pl.BlockSpec(shape, lambda b, pt: (0,) * len(shape))
    grid_spec = pltpu.PrefetchScalarGridSpec(
        num_scalar_prefetch=1,
        grid=(n_req // rps,),
        in_specs=[pl.BlockSpec((rps, w), lambda b, pt: (b, 0)),
                  const((w, w)), const((w, w)),
                  pl.BlockSpec(memory_space=pl.ANY)],
        out_specs=[pl.BlockSpec((rows, w), lambda b, pt: (b, 0)),
                   pl.BlockSpec((rps, w), lambda b, pt: (b, 0)),
                   pl.BlockSpec((rps, w), lambda b, pt: (b, 0))],
        scratch_shapes=[pltpu.VMEM((rows, w), F32), pltpu.SemaphoreType.DMA((1,))])
    cum, cum_new, tot = pl.pallas_call(
        functools.partial(_fox_cum_kernel, layer=layer, n_pages=n_pages, rps=rps),
        grid_spec=grid_spec,
        out_shape=[jax.ShapeDtypeStruct((n_req * n_pages, w), F32),
                   jax.ShapeDtypeStruct((n_req, w), F32),
                   jax.ShapeDtypeStruct((n_req, w), F32)],
        compiler_params=_cparams(("arbitrary",)),
        name="fox_cum",
    )(page_table, lf_new, wc, wt, lcache)
    return cum.reshape(n_req, n_pages, w), cum_new.reshape(n_req, 1, w), tot.reshape(n_req, 1, w)


def _fox_sample_kernel(pt_ref, q_ref, kn_ref, vn_ref, *refs, n_pages, grp, n_heads, scale):
    k_refs, v_refs = refs[:grp], refs[grp:2 * grp]
    cum_ref, cumn_ref, tot_ref, o_ref, m_ref, l_ref, acc_ref = refs[2 * grp:]
    p = pl.program_id(1)
    n_steps = n_pages // grp
    nq = q_ref.shape[0]
    rows = n_heads * nq
    w = k_refs[0].shape[0]
    q = q_ref[...] * scale
    qall = jnp.concatenate([q[:, h * LANES:(h + 1) * LANES] for h in range(n_heads)], axis=0)
    lane = lax.broadcasted_iota(jnp.int32, (rows, w), 1)
    rowi = lax.broadcasted_iota(jnp.int32, (rows, w), 0)
    own_head = (lane % n_heads) == (rowi // nq)
    tot = tot_ref[...]

    @pl.when(p == 0)
    def _():
        m_ref[...] = jnp.full(m_ref.shape, NEG_INF, F32)
        l_ref[...] = jnp.zeros(l_ref.shape, F32)
        acc_ref[...] = jnp.zeros(acc_ref.shape, F32)

    def update(scores, values):
        m_old = m_ref[...]
        m_new = m_old
        for s in scores:
            m_new = jnp.maximum(m_new, jnp.max(s, axis=1, keepdims=True))
        alpha = jnp.exp(m_old - m_new)
        l_new = alpha * l_ref[...]
        pv = None
        for s, v in zip(scores, values):
            pr = jnp.exp(s - m_new)
            l_new = l_new + jnp.sum(pr, axis=1, keepdims=True)
            d = _dot(pr, v)
            pv = d if pv is None else pv + d
        m_ref[...] = m_new
        l_ref[...] = l_new
        acc_ref[...] = alpha * acc_ref[...] + pv

    @pl.when(p < n_steps)
    def _():
        scores = []
        for g in range(grp):
            ck = cum_ref[pl.ds(p * grp + g, 1), :]
            s = _dot_nt(qall, k_refs[g][...]) + (tot - ck)
            scores.append(jnp.where(own_head, s, NEG_INF))
        update(scores, [v_refs[g][...] for g in range(grp)])

    @pl.when(p == n_steps)
    def _():
        s = _dot_nt(qall, _pad_rows(kn_ref[...], w)) + (tot - cumn_ref[...])
        causal = jnp.logical_and(own_head, (lane // n_heads) <= (rowi % nq))
        update([jnp.where(causal, s, NEG_INF)], [_pad_rows(vn_ref[...], w)])
        out = acc_ref[...] / l_ref[...]
        for h in range(n_heads):
            o_ref[:, h * LANES:(h + 1) * LANES] = out[h * nq:(h + 1) * nq].astype(o_ref.dtype)


def _fox_sample(page_table, q, kn2, vn2, kcache, vcache, cum_past, cum_new, tot, layer, n_heads):
    n_req, n_pages = page_table.shape
    m, width = q.shape
    nq = m // n_req
    w = kcache.shape[2]
    grp = 16 if n_pages % 16 == 0 else (4 if n_pages % 4 == 0 else 1)
    n_steps = n_pages // grp

    def page_spec(g):
        return pl.BlockSpec((None, None, w, LANES),
                            lambda b, p, pt: (layer, pt[b, jnp.minimum(p, n_steps - 1) * grp + g], 0, 0))

    new_spec = pl.BlockSpec((nq * n_heads, LANES), lambda b, p, pt: (b, 0))
    grid_spec = pltpu.PrefetchScalarGridSpec(
        num_scalar_prefetch=1,
        grid=(n_req, n_steps + 1),
        in_specs=[pl.BlockSpec((nq, width), lambda b, p, pt: (b, 0)), new_spec, new_spec]
                 + [page_spec(g) for g in range(grp)] + [page_spec(g) for g in range(grp)]
                 + [pl.BlockSpec((None, n_pages, w), lambda b, p, pt: (b, 0, 0)),
                    pl.BlockSpec((None, 1, w), lambda b, p, pt: (b, 0, 0)),
                    pl.BlockSpec((None, 1, w), lambda b, p, pt: (b, 0, 0))],
        out_specs=pl.BlockSpec((nq, width), lambda b, p, pt: (b, 0)),
        scratch_shapes=[pltpu.VMEM((n_heads * nq, 1), F32), pltpu.VMEM((n_heads * nq, 1), F32),
                        pltpu.VMEM((n_heads * nq, LANES), F32)])
    return pl.pallas_call(
        functools.partial(_fox_sample_kernel, n_pages=n_pages, grp=grp, n_heads=n_heads, scale=LANES ** -0.5),
        grid_spec=grid_spec,
        out_shape=jax.ShapeDtypeStruct((m, width), F32),
        compiler_params=_cparams(("arbitrary", "arbitrary")),
        name="fox_sample",
    )(page_table, q, kn2, vn2, *([kcache] * grp), *([vcache] * grp), cum_past, cum_new, tot)


def _block_diag_gates(wa, wx, grp):
    g, w, _ = wa.shape
    eye = jnp.eye(grp, dtype=wa.dtype)

    def bd(x):
        x = x.reshape(g // grp, grp, w, w)
        return jnp.einsum("cgij,gh->cgihj", x, eye).reshape(g // grp, grp * w, grp * w)

    return jnp.concatenate([bd(wa), bd(wx)], axis=2)


def _layer(xs, l, wts, dims, lru_state, sample, prev_kv):
    (norm1_g, w_in, w_gate, b_gate, wf_t, fox_bf, conv_w, conv_b, wg, lru_ba, lru_bx, lru_lambda,
     w_br_lru, w_br_sb, w_br_fox, w_o, norm2_g, w_up, w_down) = wts
    depth = w_in.shape[0]
    d = xs[0].shape[1]
    r = lru_lambda.shape[1]
    width = w_br_sb.shape[1]
    h = width // LANES
    off_sb = 2 * r
    off_fox = off_sb + 3 * width
    tms = [min(x.shape[0], 1024) for x in xs]

    xn = [_rmsnorm(x, norm1_g[l], BF16) for x in xs]
    grp = lambda acts, extras=None: [([a[g] for a in acts], [e[g] for e in (extras or [])], tms[g])
                                     for g in range(len(xs))]
    proj = lambda off, n, nm, **kw: _gmm(grp([xn]), [w_in], l, n, tn=1024, epilogue=_epi_plain, out_dtype=F32,
                                         col_off=off, name=nm, **kw)
    u_lru = proj(0, 2 * r, "proj_lru")
    sb_q = proj(off_sb, width, "proj_sbq")
    fox_q = proj(off_fox, width, "proj_fq")
    kv = []
    for idx, (off, nm) in enumerate([(off_sb + width, "proj_sbk"), (off_sb + 2 * width, "proj_sbv"),
                                     (off_fox + width, "proj_fk"), (off_fox + 2 * width, "proj_fv")]):
        tms_kv = [min(t, 512) for t in tms] if prev_kv is None else tms
        kv.append(_gmm([([xn[g]], [], tms_kv[g]) for g in range(len(xs))], [w_in], l, width, tn=1024,
                       epilogue=_epi_plain, out_dtype=F32, col_off=off, name=nm,
                       stack=(depth, None if prev_kv is None else prev_kv[idx])))
    gates = _gmm(grp([xn]), [w_gate], l, 3 * d, tn=1024, epilogue=_epi_sigmoid, out_dtype=F32,
                 bias=b_gate[l].reshape(1, 3 * d), name="proj_gate")

    branches, states = [], []
    for g, (n_seq, seq_len) in enumerate(dims):
        lf_t, cum_t = _forget(xn[g], wf_t[l], fox_bf[l], seq_len)
        h0, conv0 = lru_state[g]
        y_lru, h_last, conv_new = _lru(u_lru[g], h0, conv0, conv_w[l], conv_b[l], wg[l], lru_ba[l], lru_bx[l],
                                       lru_lambda[l], n_seq, seq_len)
        lf = lf_t.T
        if g == 0:
            o_sb = _prompt_attention(_sb_prompt_kernel, sb_q[g], kv[0][g], kv[1][g], l, n_seq, seq_len,
                                     name="sb_prompt")
            cum = cum_t.reshape(h, n_seq, 1, seq_len).transpose(1, 0, 2, 3)
            o_fox = _prompt_attention(_fox_prompt_kernel, fox_q[g], kv[2][g], kv[3][g], l, n_seq, seq_len,
                                      cum=cum, bk_max=512, name="fox_prompt")
        else:
            page_table, c_sb_k, c_sb_v, c_fox_k, c_fox_v, c_logf, page = sample
            o_sb = _sb_sample(page_table, sb_q[g], kv[0][g], kv[1][g], c_sb_k, c_sb_v, l, h)
            lf_new = jnp.pad(lf.reshape(n_seq, seq_len * h), ((0, 0), (0, (page - seq_len) * h)))
            cum_past, cum_new, tot = _fox_cum(page_table, lf_new, c_logf, l, page, h)
            rows = lambda a: a[l].reshape(n_seq * seq_len * h, LANES)
            o_fox = _fox_sample(page_table, fox_q[g], rows(kv[2][g]), rows(kv[3][g]), c_fox_k, c_fox_v,
                                cum_past, cum_new, tot, l, h)
        branches.append((y_lru.astype(BF16), o_sb.astype(BF16), o_fox.astype(BF16)))
        states.append((lf.reshape(n_seq, seq_len, h), h_last, conv_new))

    acts = [[b[i] for b in branches] for i in range(3)]
    nb = d // 512
    tms_mid = [min(t, 512) for t in tms]
    merged = _gmm([([a[g] for a in acts], [gates[g]] * 3, tms_mid[g]) for g in range(len(xs))],
                  [w_br_lru, w_br_sb, w_br_fox], l, d, tn=512, epilogue=_epi_gated_sum,
                  out_dtype=BF16, extra_offs=(0, nb, 2 * nb), name="merge")
    x1 = _gmm([([merged[g]], [xs[g]], tms_mid[g]) for g in range(len(xs))], [w_o], l, d, tn=1024,
              epilogue=_epi_residual, out_dtype=F32, extra_offs=(0,), name="out_proj")
    hn = [_rmsnorm(x, norm2_g[l], BF16) for x in x1]
    hid = _gmm(grp([hn]), [w_up], l, w_up.shape[2], tn=1024, epilogue=_epi_relu2, out_dtype=BF16, name="mlp_up")
    tms_down = [min(t, 256) for t in tms]
    x2 = _gmm([([hid[g]], [x1[g]], tms_down[g]) for g in range(len(xs))], [w_down], l, d, tn=512,
              epilogue=_epi_residual, out_dtype=F32, extra_offs=(0,), single_buffer_w=True, name="mlp_down")
    return x2, kv, states


def kernel(x_prompt, x_sample, cache_sb_k, cache_sb_v, cache_fox_k, cache_fox_v, cache_fox_logf, state_lru_h, state_conv, page_table, norm1_g, w_in, b_gate, conv_w, conv_b, lru_wa, lru_ba, lru_wx, lru_bx, lru_lambda, fox_bf, w_br_lru, w_br_sb, w_br_fox, w_o, norm2_g, w_up, w_down, final_g):
    depth = w_in.shape[0]
    n_p, t_p, d = x_prompt.shape
    n_s, t_s, _ = x_sample.shape
    r = lru_lambda.shape[1]
    n_pool, page, h_sb, hd = cache_sb_k.shape[1:]
    h_fox = cache_fox_k.shape[3]
    assert hd == LANES and h_sb == h_fox == SUBLANES
    width = h_sb * hd
    off_f = 2 * r + 6 * width
    off_gate = off_f + h_fox

    w_gate = w_in[:, :, off_gate:]
    wf_t = jnp.pad(jnp.swapaxes(w_in[:, :, off_f:off_gate], 1, 2), ((0, 0), (0, 2 * SUBLANES - h_fox), (0, 0)))
    wg = jax.vmap(lambda a, b: _block_diag_gates(a, b, 4))(lru_wa, lru_wx)
    wts = (norm1_g, w_in, w_gate, b_gate, wf_t, fox_bf, conv_w, conv_b, wg, lru_ba, lru_bx, lru_lambda,
           w_br_lru, w_br_sb, w_br_fox, w_o, norm2_g, w_up, w_down)

    rows = lambda c: c.reshape(depth, n_pool, page * h_sb, hd)
    sample = (page_table, rows(cache_sb_k), rows(cache_sb_v), rows(cache_fox_k), rows(cache_fox_v),
              cache_fox_logf.reshape(depth, n_pool, 1, page * h_fox), page)

    xs = [x_prompt.reshape(n_p * t_p, d), x_sample.reshape(n_s * t_s, d)]
    dims = [(n_p, t_p), (n_s, t_s)]
    kv = None
    small = []
    for l in range(depth):
        lru_state = [(jnp.zeros((n_p, r), F32), jnp.zeros((n_p, conv_w.shape[1] - 1, r), F32)),
                     (state_lru_h[l], state_conv[l])]
        xs, kv, states = _layer(xs, l, wts, dims, lru_state, sample, kv)
        small.append(states)

    y = [_rmsnorm(x, final_g, F32).reshape(n, t, d) for x, (n, t) in zip(xs, dims)]
    outs = []
    for g, (n, t) in enumerate(dims):
        outs.extend(a[g].reshape(depth, n, t, h_sb, hd) for a in kv)
        outs.extend(jnp.stack([small[l][g][i] for l in range(depth)]) for i in range(3))
    return (y[0], y[1], *outs)
```
